```python
import math
import numpy as np
import jax
import jax.numpy as jnp
from jax import lax

D_MODEL = 2048
BATCH = 16
SEQ = 2048
DEPTH = 4
DEC_BATCH = 2
DEC_SEQ = 16384
PAST_LEN = 128

GRID_W = 64
EPS = 1e-6
N_BRANCH = 4
BRANCH_W = 1024
SSD_HEAD_DIM = 64
SSD_HEADS = BRANCH_W // SSD_HEAD_DIM
SSD_GROUPS = 2
SSD_STATE = 128
SSD_CONV = 5
SSD_CHUNK = 128
SSD_XBC = BRANCH_W + 2 * SSD_GROUPS * SSD_STATE
GQA_HEAD_DIM = 128
GQA_HEADS = BRANCH_W // GQA_HEAD_DIM
GQA_KV_HEADS = 2
Q_BLOCK = 128
ROPE_THETA = 10000.0
NA_HEAD_DIM = 64
NA_HEADS = BRANCH_W // NA_HEAD_DIM
NA_WIN_ROWS = 8
NA_WIN_COLS = 16
ML_HEADS = 4
ML_HEAD_DIM = BRANCH_W // ML_HEADS
ML_CHUNK = 128
FFN_HIDDEN = -(-8 * D_MODEL // (3 * 256)) * 256
SPLIT_SIZES = (
    BRANCH_W, SSD_XBC, 2 * SSD_HEADS,
    GQA_HEADS * GQA_HEAD_DIM, GQA_KV_HEADS * GQA_HEAD_DIM, GQA_KV_HEADS * GQA_HEAD_DIM,
    BRANCH_W, BRANCH_W, BRANCH_W,
    BRANCH_W, BRANCH_W, BRANCH_W, BRANCH_W, 4 * ML_HEADS,
)
IN_COLS = sum(SPLIT_SIZES)

kernel_name = 'hybrid_bidir_ssd_gqa_natten_mlstm_encoder'


def rms_norm(x, w):
    xf = x.astype(jnp.float32)
    y = xf * lax.rsqrt(jnp.mean(xf * xf, axis=-1, keepdims=True) + EPS)
    return (y * w).astype(x.dtype)


def flip_seq(t):
    return jnp.flip(t, axis=1)


def centred_dwconv(x, w, b):
    ch = x.shape[-1]
    k = w.shape[0]
    y = lax.conv_general_dilated(x, w[:, None, :].astype(x.dtype), window_strides=(1,),
                                 padding=[(k // 2, k // 2)], dimension_numbers=('NWC', 'WIO', 'NWC'),
                                 feature_group_count=ch)
    return y + b


def ssd_scan(x, dt, a, bm, cm):
    b, s, h, p = x.shape
    g, n = bm.shape[-2:]
    r = h // g
    L = SSD_CHUNK
    nc = s // L
    da = (dt * a).astype(jnp.float32).reshape(b, nc, L, g, r)
    xdt = (x * dt[..., None]).reshape(b, nc, L, g, r, p)
    bm = bm.reshape(b, nc, L, g, n)
    cm = cm.reshape(b, nc, L, g, n)
    acs = jnp.cumsum(da, axis=2)
    causal = np.tril(np.ones((L, L), dtype=bool))[:, :, None, None]
    decay = jnp.exp(jnp.where(causal, acs[:, :, :, None] - acs[:, :, None, :], -jnp.inf))
    cb = jnp.einsum('bclgn,bcsgn->bclsg', cm, bm)
    y_diag = jnp.einsum('bclsg,bclsgr,bcsgrp->bclgrp', cb, decay, xdt)
    dstate = jnp.exp(acs[:, :, -1:] - acs)
    states = jnp.einsum('bclgn,bclgr,bclgrp->bcgrpn', bm, dstate, xdt)
    chunk_decay = jnp.exp(acs[:, :, -1])

    def step(carry, inp):
        st, dec = inp
        return dec[..., None, None] * carry + st, carry

    init = jnp.zeros_like(states[:, 0])
    _, prev = lax.scan(step, init, (jnp.moveaxis(states, 1, 0), jnp.moveaxis(chunk_decay, 1, 0)))
    prev = jnp.moveaxis(prev, 0, 1)
    y_off = jnp.einsum('bclgn,bcgrpn,bclgr->bclgrp', cm, prev, jnp.exp(acs))
    return (y_diag + y_off).reshape(b, s, h, p)


def ssd_mixer(z, xbc, dt_raw, conv_w, conv_b, dt_bias, a_log, d_skip, norm_w):
    b, s, _ = z.shape
    xbc = jax.nn.silu(centred_dwconv(xbc, conv_w, conv_b))
    xs, bm, cm = jnp.split(xbc, [BRANCH_W, BRANCH_W + SSD_GROUPS * SSD_STATE], axis=-1)
    xs = xs.reshape(b, s, SSD_HEADS, SSD_HEAD_DIM)
    bm = bm.reshape(b, s, SSD_GROUPS, SSD_STATE)
    cm = cm.reshape(b, s, SSD_GROUPS, SSD_STATE)
    dt = jax.nn.softplus(dt_raw.astype(jnp.float32).reshape(b, s, 2, SSD_HEADS) + dt_bias)
    a = -jnp.exp(a_log.astype(jnp.float32))
    y_f = ssd_scan(xs, dt[:, :, 0], a[0], bm, cm)
    y_b = flip_seq(ssd_scan(flip_seq(xs), flip_seq(dt[:, :, 1]), a[1], flip_seq(bm), flip_seq(cm)))
    y = y_f + y_b + d_skip[:, None] * xs
    y = y.reshape(b, s, BRANCH_W) * jax.nn.silu(z)
    return rms_norm(y, norm_w)


def rope_1d(t, pos):
    d = t.shape[-1]
    inv = np.asarray(ROPE_THETA ** (-np.arange(0, d, 2, dtype=np.float32) / d), dtype=np.float32)
    ang = pos[:, None].astype(np.float32) * inv[None]
    cos = np.cos(ang)[:, None, :]
    sin = np.sin(ang)[:, None, :]
    t1, t2 = t[..., : d // 2], t[..., d // 2:]
    return jnp.concatenate([t1 * cos - t2 * sin, t1 * sin + t2 * cos], axis=-1).astype(t.dtype)


def axial_rope(t, rows, cols):
    half = t.shape[-1] // 2
    return jnp.concatenate([rope_1d(t[..., :half], rows), rope_1d(t[..., half:], cols)], axis=-1)


def gqa_mixer(q, k, v, q_norm, k_norm):
    b, s, _ = q.shape
    grp = GQA_HEADS // GQA_KV_HEADS
    q = rms_norm(q.reshape(b, s, GQA_HEADS, GQA_HEAD_DIM), q_norm)
    k = rms_norm(k.reshape(b, s, GQA_KV_HEADS, GQA_HEAD_DIM), k_norm)
    v = v.reshape(b, s, GQA_KV_HEADS, GQA_HEAD_DIM)
    pos = np.arange(s)
    rows, cols = pos // GRID_W, pos % GRID_W
    q = axial_rope(q, rows, cols)
    k = axial_rope(k, rows, cols)
    nb = s // Q_BLOCK
    qb = q.reshape(b, nb, Q_BLOCK, GQA_KV_HEADS, grp, GQA_HEAD_DIM).transpose(1, 0, 2, 3, 4, 5)
    scale = GQA_HEAD_DIM ** -0.5

    def block(qblk):
        sc = jnp.einsum('bqkgd,bskd->bkgqs', qblk, k).astype(jnp.float32) * scale
        p = jax.nn.softmax(sc, axis=-1).astype(v.dtype)
        return jnp.einsum('bkgqs,bskd->bqkgd', p, v)

    o = lax.map(block, qb)
    return o.transpose(1, 0, 2, 3, 4, 5).reshape(b, s, BRANCH_W)


def na_mixer(q, k, v, q_norm, k_norm, rpb):
    b, s, _ = q.shape
    rows = s // GRID_W
    wr = min(NA_WIN_ROWS, rows)
    q = rms_norm(q.reshape(b, rows, GRID_W, NA_HEADS, NA_HEAD_DIM), q_norm)
    k = rms_norm(k.reshape(b, rows, GRID_W, NA_HEADS, NA_HEAD_DIM), k_norm)
    v = v.reshape(b, rows, GRID_W, NA_HEADS, NA_HEAD_DIM)
    j = np.arange(GRID_W)
    col_start = np.clip(j - NA_WIN_COLS // 2, 0, GRID_W - NA_WIN_COLS)
    col_idx = col_start[:, None] + np.arange(NA_WIN_COLS)[None, :]
    dc_idx = col_idx - j[:, None] + (NA_WIN_COLS - 1)
    rpb_c = rpb[:, :, dc_idx]
    scale = NA_HEAD_DIM ** -0.5

    def row_block(r):
        rs = jnp.clip(r - wr // 2, 0, rows - wr)
        kb = lax.dynamic_slice_in_dim(k, rs, wr, axis=1)[:, :, col_idx]
        vb = lax.dynamic_slice_in_dim(v, rs, wr, axis=1)[:, :, col_idx]
        qr = lax.dynamic_index_in_dim(q, r, axis=1, keepdims=False)
        dr_idx = rs + jnp.arange(wr) - r + (NA_WIN_ROWS - 1)
        bias = jnp.take(rpb_c, dr_idx, axis=1).transpose(0, 2, 1, 3)
        sc = jnp.einsum('bjhd,bajchd->bhjac', qr, kb).astype(jnp.float32) * scale + bias
        p = jax.nn.softmax(sc.reshape(b, NA_HEADS, GRID_W, wr * NA_WIN_COLS), axis=-1)
        p = p.reshape(sc.shape).astype(vb.dtype)
        return jnp.einsum('bhjac,bajchd->bjhd', p, vb)

    o = lax.map(row_block, jnp.arange(rows))
    return o.transpose(1, 0, 2, 3, 4).reshape(b, s, BRANCH_W)


def mlstm_scan(q, k, v, li, lf):
    b, s, h, d = q.shape
    L = ML_CHUNK
    nc = s // L
    q = q.reshape(b, nc, L, h, d)
    k = k.reshape(b, nc, L, h, d)
    v = v.reshape(b, nc, L, h, d)
    li = li.reshape(b, nc, L, h)
    bcs = jnp.cumsum(lf.reshape(b, nc, L, h), axis=2)
    causal = np.tril(np.ones((L, L), dtype=bool))
    bt = bcs.transpose(0, 1, 3, 2)
    dmat = bt[..., :, None] - (bcs - li).transpose(0, 1, 3, 2)[..., None, :]
    dmat = jnp.where(causal, dmat, -jnp.inf)
    g_end = bcs[:, :, -1:] - bcs + li
    m_loc = jnp.max(g_end, axis=2)
    w_end = jnp.exp(g_end - m_loc[:, :, None])
    c_loc = jnp.einsum('bclh,bclhd,bclhe->bchde', w_end, k, v)
    n_loc = jnp.einsum('bclh,bclhd->bchd', w_end, k)
    b_end = bcs[:, :, -1]

    def step(carry, inp):
        c_prev, n_prev, m_prev = carry
        c_l, n_l, m_l, be = inp
        m_new = jnp.maximum(be + m_prev, m_l)
        a_old = jnp.exp(be + m_prev - m_new)
        a_loc = jnp.exp(m_l - m_new)
        c_new = a_old[..., None, None] * c_prev + a_loc[..., None, None] * c_l
        n_new = a_old[..., None] * n_prev + a_loc[..., None] * n_l
        return (c_new, n_new, m_new), (c_prev, n_prev, m_prev)

    init = (jnp.zeros_like(c_loc[:, 0]), jnp.zeros_like(n_loc[:, 0]), jnp.zeros_like(m_loc[:, 0]))
    mv = lambda t: jnp.moveaxis(t, 1, 0)
    _, (c_prev, n_prev, m_prev) = lax.scan(step, init, (mv(c_loc), mv(n_loc), mv(m_loc), mv(b_end)))
    c_prev, n_prev, m_prev = mv(c_prev), mv(n_prev), mv(m_prev)
    inter_log = bt + m_prev[..., None]
    m_t = jnp.maximum(inter_log, jnp.max(dmat, axis=-1))
    w_intra = jnp.exp(dmat - m_t[..., None]) * jnp.einsum('bclhd,bcshd->bchls', q, k)
    w_inter = jnp.exp(inter_log - m_t)
    num = (jnp.einsum('bchls,bcshe->bclhe', w_intra, v)
           + w_inter.transpose(0, 1, 3, 2)[..., None] * jnp.einsum('bclhd,bchde->bclhe', q, c_prev))
    den = jnp.sum(w_intra, axis=-1) + w_inter * jnp.einsum('bclhd,bchd->bchl', q, n_prev)
    den = jnp.maximum(jnp.abs(den), jnp.exp(-m_t)).transpose(0, 1, 3, 2)[..., None]
    return (num / den).reshape(b, s, h, d)


def mlstm_mixer(q, k, v, o, gates, gate_bias, norm_w):
    b, s, _ = q.shape
    sh = lambda t: t.reshape(b, s, ML_HEADS, ML_HEAD_DIM)
    q, k, v = sh(q), sh(k) * (ML_HEAD_DIM ** -0.5), sh(v)
    gp = gates.astype(jnp.float32).reshape(b, s, 4, ML_HEADS) + gate_bias
    li_f, lf_f = gp[:, :, 0], jax.nn.log_sigmoid(gp[:, :, 1])
    li_b, lf_b = gp[:, :, 2], jax.nn.log_sigmoid(gp[:, :, 3])
    h_f = mlstm_scan(q, k, v, li_f, lf_f)
    h_b = flip_seq(mlstm_scan(flip_seq(q), flip_seq(k), flip_seq(v), flip_seq(li_b), flip_seq(lf_b)))
    hh = rms_norm(h_f + h_b, norm_w.reshape(ML_HEADS, ML_HEAD_DIM)).reshape(b, s, BRANCH_W)
    return jax.nn.sigmoid(o) * hh


def encoder_layer(x, c, w_ada, b_ada, norm1_w, w_in, w_bgate, b_bgate, ssd_conv_w, ssd_conv_b,
                  ssd_dt_bias, ssd_a_log, ssd_d, ssd_norm_w, gqa_q_norm, gqa_k_norm, na_q_norm, na_k_norm,
                  na_rpb, ml_gate_bias, ml_out_norm, w_branch_out, w_o, norm2_w, w_ffn_in, w_ffn_out):
    mod = (jax.nn.silu(c) @ w_ada + b_ada)[:, None, :]
    sh1, sc1, g1, sh2, sc2, g2 = jnp.split(mod, 6, axis=-1)
    h = rms_norm(x, norm1_w) * (1 + sc1) + sh1
    parts = jnp.split(h @ w_in, np.cumsum(SPLIT_SIZES)[:-1].tolist(), axis=-1)
    z, xbc, dt_raw, gq, gk, gv, nq, nk, nv, mq, mk, mvv, mo, mg = parts
    ys = (
        ssd_mixer(z, xbc, dt_raw, ssd_conv_w, ssd_conv_b, ssd_dt_bias, ssd_a_log, ssd_d, ssd_norm_w),
        gqa_mixer(gq, gk, gv, gqa_q_norm, gqa_k_norm),
        na_mixer(nq, nk, nv, na_q_norm, na_k_norm, na_rpb),
        mlstm_mixer(mq, mk, mvv, mo, mg, ml_gate_bias, ml_out_norm),
    )
    gates = jax.nn.sigmoid((h @ w_bgate + b_bgate).astype(jnp.float32)).astype(h.dtype)
    gates = jnp.split(gates, N_BRANCH, axis=-1)
    merged = gates[0] * (ys[0].astype(h.dtype) @ w_branch_out[0])
    for i in range(1, N_BRANCH):
        merged = merged + gates[i] * (ys[i].astype(h.dtype) @ w_branch_out[i])
    x = x + g1 * (merged @ w_o)
    h2 = rms_norm(x, norm2_w) * (1 + sc2) + sh2
    up, gate = jnp.split(h2 @ w_ffn_in, 2, axis=-1)
    x = x + g2 * ((jax.nn.silu(gate) * up) @ w_ffn_out)
    return x


def trunk(x, c, layer_weights):
    for l in range(DEPTH):
        x = encoder_layer(x, c, *[w[l] for w in layer_weights])
    return x


def setup_inputs(seed: int = 0) -> dict:
    key = jax.random.key(seed)
    ks = jax.random.split(key, 32)
    f32 = jnp.float32

    def nrm(k, shape, fan_in, mult=1.0):
        return jax.random.normal(k, shape, f32) * (mult * fan_in ** -0.5)

    def gain(k, shape):
        return 1.0 + 0.05 * jax.random.normal(k, shape, f32)

    dt = jnp.exp(jax.random.uniform(ks[10], (DEPTH, 2, SSD_HEADS), f32)
                 * (math.log(0.1) - math.log(0.001)) + math.log(0.001))
    gate_base = jnp.array([0.0, 3.0, 0.0, 3.0], f32)[None, :, None]
    return {
        'x_prompt': jax.random.normal(ks[0], (BATCH, SEQ, D_MODEL), f32),
        'x_sample': jax.random.normal(ks[1], (DEC_BATCH, DEC_SEQ, D_MODEL), f32),
        'c_prompt': jax.random.normal(ks[2], (BATCH, D_MODEL), f32),
        'c_sample': jax.random.normal(ks[3], (DEC_BATCH, D_MODEL), f32),
        'w_ada': nrm(ks[4], (DEPTH, D_MODEL, 6 * D_MODEL), D_MODEL, 0.5),
        'b_ada': 0.02 * jax.random.normal(ks[5], (DEPTH, 6 * D_MODEL), f32),
        'norm1_w': gain(ks[6], (DEPTH, D_MODEL)),
        'w_in': nrm(ks[7], (DEPTH, D_MODEL, IN_COLS), D_MODEL),
        'w_bgate': nrm(ks[8], (DEPTH, D_MODEL, N_BRANCH * D_MODEL), D_MODEL),
        'b_bgate': 0.02 * jax.random.normal(ks[9], (DEPTH, N_BRANCH * D_MODEL), f32),
        'ssd_conv_w': nrm(ks[11], (DEPTH, SSD_CONV, SSD_XBC), SSD_CONV),
        'ssd_conv_b': 0.02 * jax.random.normal(ks[12], (DEPTH, SSD_XBC), f32),
        'ssd_dt_bias': dt + jnp.log(-jnp.expm1(-dt)),
        'ssd_a_log': jnp.log(jax.random.uniform(ks[13], (DEPTH, 2, SSD_HEADS), f32, 1.0, 16.0)),
        'ssd_d': 1.0 + 0.1 * jax.random.normal(ks[14], (DEPTH, SSD_HEADS), f32),
        'ssd_norm_w': gain(ks[15], (DEPTH, BRANCH_W)),
        'gqa_q_norm': gain(ks[16], (DEPTH, GQA_HEAD_DIM)),
        'gqa_k_norm': gain(ks[17], (DEPTH, GQA_HEAD_DIM)),
        'na_q_norm': gain(ks[18], (DEPTH, NA_HEAD_DIM)),
        'na_k_norm': gain(ks[19], (DEPTH, NA_HEAD_DIM)),
        'na_rpb': 0.1 * jax.random.normal(ks[20], (DEPTH, NA_HEADS, 2 * NA_WIN_ROWS - 1, 2 * NA_WIN_COLS - 1), f32),
        'ml_gate_bias': gate_base + 0.3 * jax.random.normal(ks[21], (DEPTH, 4, ML_HEADS), f32),
        'ml_out_norm': gain(ks[22], (DEPTH, BRANCH_W)),
        'w_branch_out': nrm(ks[23], (DEPTH, N_BRANCH, BRANCH_W, D_MODEL), BRANCH_W),
        'w_o': nrm(ks[24], (DEPTH, D_MODEL, D_MODEL), D_MODEL),
        'norm2_w': gain(ks[25], (DEPTH, D_MODEL)),
        'w_ffn_in': nrm(ks[26], (DEPTH, D_MODEL, 2 * FFN_HIDDEN), D_MODEL),
        'w_ffn_out': nrm(ks[27], (DEPTH, FFN_HIDDEN, D_MODEL), FFN_HIDDEN),
    }


def reference(x_prompt, x_sample, c_prompt, c_sample, w_ada, b_ada, norm1_w, w_in, w_bgate, b_bgate,
              ssd_conv_w, ssd_conv_b, ssd_dt_bias, ssd_a_log, ssd_d, ssd_norm_w, gqa_q_norm, gqa_k_norm,
              na_q_norm, na_k_norm, na_rpb, ml_gate_bias, ml_out_norm, w_branch_out, w_o, norm2_w,
              w_ffn_in, w_ffn_out):
    layer_weights = (w_ada, b_ada, norm1_w, w_in, w_bgate, b_bgate, ssd_conv_w, ssd_conv_b, ssd_dt_bias,
                     ssd_a_log, ssd_d, ssd_norm_w, gqa_q_norm, gqa_k_norm, na_q_norm, na_k_norm, na_rpb,
                     ml_gate_bias, ml_out_norm, w_branch_out, w_o, norm2_w, w_ffn_in, w_ffn_out)
    y_prompt = trunk(x_prompt, c_prompt, layer_weights)
    y_sample = trunk(x_sample, c_sample, layer_weights)
    return (y_prompt, y_sample)
```

```python
import functools
import math

import numpy as np
import jax
import jax.numpy as jnp
from jax import lax
from jax.experimental import pallas as pl
from jax.experimental.pallas import tpu as pltpu

F32 = jnp.float32
BF16 = jnp.bfloat16

D_MODEL = 2048
DEPTH = 4
GRID_W = 64
EPS = 1e-6
N_BRANCH = 4
BRANCH_W = 1024
SSD_HEAD_DIM = 64
SSD_HEADS = 16
SSD_GROUPS = 2
SSD_STATE = 128
SSD_CONV = 5
CHUNK = 128
GQA_HEAD_DIM = 128
GQA_HEADS = 8
GQA_KV_HEADS = 2
ROPE_THETA = 10000.0
NA_HEAD_DIM = 64
NA_HEADS = 16
NA_WIN_ROWS = 8
NA_WIN_COLS = 16
ML_HEADS = 4
ML_HEAD_DIM = 256
FFN_HIDDEN = 5632

VMEM_LIMIT_BYTES = 56 * 1024 * 1024
LANES = 128

C_GATE = 0
C_Z = 8192
C_XS = 9216
C_GQ = 10240
C_NQ = 11264
C_NK = 12288
C_NV = 13312
C_MQ = 14336
C_MK = 15360
C_MV = 16384
C_MO = 17408
C_BM = 18432
C_GK = 18944
C_GV = 19200
BIG_COLS = 19456
S_DT = 0
S_MG = 32
NEG = -1e30


def _cparams(sem):
    return pltpu.CompilerParams(dimension_semantics=sem, vmem_limit_bytes=VMEM_LIMIT_BYTES)


def _silu(x):
    return x * jax.nn.sigmoid(x)


def _softplus(x):
    return jnp.maximum(x, 0.0) + jnp.log1p(jnp.exp(-jnp.abs(x)))


def _split3(x):
    x1 = x.astype(BF16)
    r1 = x - x1.astype(F32)
    x2 = r1.astype(BF16)
    x3 = (r1 - x2.astype(F32)).astype(BF16)
    return x1, x2, x3


def _dot_exact_rhs(x, m):
    a, b, c = _split3(x)
    d = lambda t: jnp.dot(t, m, preferred_element_type=F32)
    return d(a) + d(b) + d(c)


def _dot_exact_lhs(m, x):
    a, b, c = _split3(x)
    d = lambda t: jnp.dot(m, t, preferred_element_type=F32)
    return d(a) + d(b) + d(c)


def _tri(n, lower):
    r = lax.broadcasted_iota(jnp.int32, (n, n), 0)
    c = lax.broadcasted_iota(jnp.int32, (n, n), 1)
    return (c <= r) if lower else (c >= r)


def _ada_kernel(c_ref, w_ref, b_ref, o_ref):
    a = _silu(c_ref[...]).astype(BF16)
    o_ref[...] = jnp.dot(a, w_ref[...], preferred_element_type=F32) + b_ref[...]


def _ada_call(c_pad, w_ada, b_ada):
    rows = c_pad.shape[0]
    tn = 2048
    nt = w_ada.shape[-1] // tn
    return pl.pallas_call(
        _ada_kernel,
        out_shape=jax.ShapeDtypeStruct((DEPTH, rows, w_ada.shape[-1]), F32),
        grid=(DEPTH, nt),
        in_specs=[
            pl.BlockSpec((rows, D_MODEL), lambda l, j: (0, 0)),
            pl.BlockSpec((None, D_MODEL, tn), lambda l, j: (l, 0, j)),
            pl.BlockSpec((None, 1, tn), lambda l, j: (l, 0, j)),
        ],
        out_specs=pl.BlockSpec((None, rows, tn), lambda l, j: (l, 0, j)),
        compiler_params=_cparams(("arbitrary", "arbitrary")),
        name="ada_mod",
    )(c_pad, w_ada, b_ada)


def _norm_mod_rows(x_ref, nw, sc, sh, r0, rc):
    xf = x_ref[pl.ds(r0, rc), :]
    ms = jnp.mean(xf * xf, axis=-1, keepdims=True)
    return xf * lax.rsqrt(ms + EPS) * nw * sc + sh


def _inproj_kernel(x_ref, mod_ref, nw_ref, w_ref, b_ref, wsh_ref, wsl_ref, big_ref, small_ref, h_scr,
                   *, tm, rc, n_gate_tiles):
    j = pl.program_id(1)

    @pl.when(j == 0)
    def _():
        nw = nw_ref[...]
        sh = mod_ref[0:1, :]
        sc = 1.0 + mod_ref[1:2, :]

        def body(r, carry):
            r0 = pl.multiple_of(r * rc, rc)
            hf = _norm_mod_rows(x_ref, nw, sc, sh, r0, rc)
            hb = hf.astype(BF16)
            h_scr[pl.ds(r0, rc), :] = hb
            lo = (hf - hb.astype(F32)).astype(BF16)
            wh = wsh_ref[...]
            sm = (jnp.dot(hb, wh, preferred_element_type=F32)
                  + jnp.dot(lo, wh, preferred_element_type=F32)
                  + jnp.dot(hb, wsl_ref[...], preferred_element_type=F32))
            small_ref[pl.ds(r0, rc), :] = sm
            return carry

        lax.fori_loop(0, tm // rc, body, 0)

    acc = jnp.dot(h_scr[...], w_ref[...], preferred_element_type=F32) + b_ref[...]

    @pl.when(j < n_gate_tiles)
    def _():
        big_ref[...] = jax.nn.sigmoid(acc).astype(BF16)

    @pl.when(j >= n_gate_tiles)
    def _():
        big_ref[...] = acc.astype(BF16)


def _inproj_call(x, mod, nw, w_big, b_big, ws_hi, ws_lo, seq_len):
    t = x.shape[0]
    tm, tn, rc = 1024, 1024, 128
    tiles_per_seq = seq_len // tm
    kern = functools.partial(_inproj_kernel, tm=tm, rc=rc, n_gate_tiles=(N_BRANCH * D_MODEL) // tn)
    return pl.pallas_call(
        kern,
        out_shape=(jax.ShapeDtypeStruct((t, BIG_COLS), BF16), jax.ShapeDtypeStruct((t, LANES), F32)),
        grid=(t // tm, BIG_COLS // tn),
        in_specs=[
            pl.BlockSpec((tm, D_MODEL), lambda i, j: (i, 0)),
            pl.BlockSpec((None, 6, D_MODEL), lambda i, j: (i // tiles_per_seq, 0, 0)),
            pl.BlockSpec((1, D_MODEL), lambda i, j: (0, 0)),
            pl.BlockSpec((D_MODEL, tn), lambda i, j: (0, j)),
            pl.BlockSpec((1, tn), lambda i, j: (0, j)),
            pl.BlockSpec((D_MODEL, LANES), lambda i, j: (0, 0)),
            pl.BlockSpec((D_MODEL, LANES), lambda i, j: (0, 0)),
        ],
        out_specs=(pl.BlockSpec((tm, tn), lambda i, j: (i, j)),
                   pl.BlockSpec((tm, LANES), lambda i, j: (i, 0))),
        scratch_shapes=[pltpu.VMEM((tm, D_MODEL), BF16)],
        compiler_params=_cparams(("arbitrary", "arbitrary")),
        name="inproj",
    )(x, mod, nw, w_big, b_big, ws_hi, ws_lo)


HALO = 16


def _conv_kernel(xs_ref, xsp_ref, xsn_ref, bc_ref, bcp_ref, bcn_ref, w1_ref, b1_ref, w2_ref, b2_ref,
                 oxs_ref, obc_ref, ext1, ext2, *, tm, rc, tiles_per_seq):
    i = pl.program_id(0)
    first = (i % tiles_per_seq) == 0
    last = (i % tiles_per_seq) == (tiles_per_seq - 1)
    pad = SSD_CONV // 2
    for main, prv, nxt, w_ref, b_ref, out, ext in (
            (xs_ref, xsp_ref, xsn_ref, w1_ref, b1_ref, oxs_ref, ext1),
            (bc_ref, bcp_ref, bcn_ref, w2_ref, b2_ref, obc_ref, ext2)):
        ext[HALO:HALO + tm, :] = main[...].astype(F32)
        ext[0:HALO, :] = jnp.where(first, 0.0, prv[...].astype(F32))
        ext[HALO + tm:2 * HALO + tm, :] = jnp.where(last, 0.0, nxt[...].astype(F32))
        for c in range(tm // rc):
            acc = jnp.zeros((rc, ext.shape[1]), F32) + b_ref[...]
            for k in range(SSD_CONV):
                s = HALO + c * rc + k - pad
                acc = acc + w_ref[k:k + 1, :] * ext[s:s + rc, :]
            out[c * rc:(c + 1) * rc, :] = _silu(acc).astype(BF16)


def _conv_call(big, w1, b1, w2, b2, seq_len):
    t = big.shape[0]
    tm, rc = 512, 64
    tiles_per_seq = seq_len // tm
    nb = t // HALO
    hb = tm // HALO
    kern = functools.partial(_conv_kernel, tm=tm, rc=rc, tiles_per_seq=tiles_per_seq)
    prev_map = lambda cb: (lambda i: (jnp.maximum(i * hb - 1, 0), cb))
    next_map = lambda cb: (lambda i: (jnp.minimum((i + 1) * hb, nb - 1), cb))
    cxs = C_XS // BRANCH_W
    cbc = C_BM // 512
    return pl.pallas_call(
        kern,
        out_shape=(jax.ShapeDtypeStruct((t, BRANCH_W), BF16), jax.ShapeDtypeStruct((t, 512), BF16)),
        grid=(t // tm,),
        in_specs=[
            pl.BlockSpec((tm, BRANCH_W), lambda i: (i, cxs)),
            pl.BlockSpec((HALO, BRANCH_W), prev_map(cxs)),
            pl.BlockSpec((HALO, BRANCH_W), next_map(cxs)),
            pl.BlockSpec((tm, 512), lambda i: (i, cbc)),
            pl.BlockSpec((HALO, 512), prev_map(cbc)),
            pl.BlockSpec((HALO, 512), next_map(cbc)),
            pl.BlockSpec((SSD_CONV, BRANCH_W), lambda i: (0, 0)),
            pl.BlockSpec((1, BRANCH_W), lambda i: (0, 0)),
            pl.BlockSpec((SSD_CONV, 512), lambda i: (0, 0)),
            pl.BlockSpec((1, 512), lambda i: (0, 0)),
        ],
        out_specs=(pl.BlockSpec((tm, BRANCH_W), lambda i: (i, 0)),
                   pl.BlockSpec((tm, 512), lambda i: (i, 0))),
        scratch_shapes=[pltpu.VMEM((tm + 2 * HALO, BRANCH_W), F32), pltpu.VMEM((tm + 2 * HALO, 512), F32)],
        compiler_params=_cparams(("arbitrary",)),
        name="ssd_conv",
    )(big, big, big, big, big, big, w1, b1, w2, b2)


def _ssd_direction(xs_ref, bc_ref, sm_ref, smt_ref, dtb_r, dtb_c, alog_r, alog_c, exp_ref, st_ref, d, backward):
    L = CHUNK
    hs = slice(SSD_HEADS * d, SSD_HEADS * (d + 1))
    dt_c = _softplus(sm_ref[...] + dtb_r[...])
    da_c = dt_c * (-jnp.exp(alog_r[...]))
    dt_r = _softplus(smt_ref[hs, :] + dtb_c[hs, :])
    da_r = dt_r * (-jnp.exp(alog_c[hs, :]))
    m_c = _tri(L, lower=not backward).astype(BF16)
    m_r = _tri(L, lower=backward).astype(BF16)
    acs_c = _dot_exact_lhs(m_c, da_c)
    acs_r = _dot_exact_rhs(da_r, m_r)
    valid = _tri(L, lower=not backward)
    acs_last = acs_c[0:1, :] if backward else acs_c[L - 1:L, :]
    w_c = dt_c * jnp.exp(acs_last - acs_c)
    e_c = jnp.exp(acs_c)
    expand = exp_ref[d]

    def widen(v):
        hi = v.astype(BF16)
        lo = (v - hi.astype(F32)).astype(BF16)
        return (jnp.dot(hi, expand, preferred_element_type=F32)
                + jnp.dot(lo, expand, preferred_element_type=F32))

    w_wide = widen(w_c)
    e_wide = widen(e_c)
    cd_wide = e_wide[0:1, :] if backward else e_wide[L - 1:L, :]

    xs = xs_ref[...]
    xs_f = xs.astype(F32)
    lane = lax.broadcasted_iota(jnp.int32, (L, LANES), 1)
    gw = BRANCH_W // SSD_GROUPS
    hpg = SSD_HEADS // SSD_GROUPS
    ys = []
    for g in range(SSD_GROUPS):
        bm = bc_ref[:, SSD_STATE * g:SSD_STATE * (g + 1)]
        cm = bc_ref[:, SSD_GROUPS * SSD_STATE + SSD_STATE * g:SSD_GROUPS * SSD_STATE + SSD_STATE * (g + 1)]
        cb = lax.dot_general(cm, bm, (((1,), (1,)), ((), ())), preferred_element_type=F32)
        state = st_ref[g]
        y_off = jnp.dot(cm, state.astype(BF16), preferred_element_type=F32)
        cols = []
        for hp in range(hpg // 2):
            x_pair = xs[:, gw * g + LANES * hp:gw * g + LANES * (hp + 1)]
            acc = None
            for sub in range(2):
                h = hpg * g + 2 * hp + sub
                hc = SSD_HEADS * d + h
                dm = acs_c[:, hc:hc + 1] - acs_r[h:h + 1, :]
                decay = jnp.exp(jnp.where(valid, dm, -jnp.inf))
                wmat = (cb * decay * dt_r[h:h + 1, :]).astype(BF16)
                keep = (lane < SSD_HEAD_DIM) if sub == 0 else (lane >= SSD_HEAD_DIM)
                xh = jnp.where(keep, x_pair, jnp.zeros_like(x_pair))
                part = jnp.dot(wmat, xh, preferred_element_type=F32)
                acc = part if acc is None else acc + part
            cols.append(acc)
        y_diag = jnp.concatenate(cols, axis=1)
        ys.append(y_diag + y_off * e_wide[:, gw * g:gw * (g + 1)])
        xw = (xs_f[:, gw * g:gw * (g + 1)] * w_wide[:, gw * g:gw * (g + 1)]).astype(BF16)
        new = lax.dot_general(bm, xw, (((0,), (0,)), ((), ())), preferred_element_type=F32)
        st_ref[g] = cd_wide[:, gw * g:gw * (g + 1)] * state + new
    return jnp.concatenate(ys, axis=1), xs_f


def _ssd_kernel(xsf_ref, bcf_ref, smf_ref, smtf_ref, xsb_ref, bcb_ref, smb_ref, smtb_ref,
                dtb_r, dtb_c, alog_r, alog_c, dskip_ref, exp_ref, yf_ref, yb_ref, stf, stb):
    @pl.when(pl.program_id(1) == 0)
    def _():
        stf[...] = jnp.zeros_like(stf)
        stb[...] = jnp.zeros_like(stb)

    y, xs_f = _ssd_direction(xsf_ref, bcf_ref, smf_ref, smtf_ref, dtb_r, dtb_c, alog_r, alog_c, exp_ref, stf,
                             0, False)
    yf_ref[...] = (y + dskip_ref[...] * xs_f).astype(BF16)
    y, _ = _ssd_direction(xsb_ref, bcb_ref, smb_ref, smtb_ref, dtb_r, dtb_c, alog_r, alog_c, exp_ref, stb,
                          1, True)
    yb_ref[...] = y.astype(BF16)


def _ssd_call(xs_c, bc_c, small, small_t, dtb_r, dtb_c, alog_r, alog_c, dskip, expand, batch, seq_len):
    t = xs_c.shape[0]
    L = CHUNK
    nc = seq_len // L
    fwd = lambda b, j: (b * nc + j, 0)
    bwd = lambda b, j: (b * nc + (nc - 1 - j), 0)
    fwd_t = lambda b, j: (0, b * nc + j)
    bwd_t = lambda b, j: (0, b * nc + (nc - 1 - j))
    const = lambda b, j: (0, 0)
    nh2 = 2 * SSD_HEADS
    return pl.pallas_call(
        _ssd_kernel,
        out_shape=(jax.ShapeDtypeStruct((t, BRANCH_W), BF16), jax.ShapeDtypeStruct((t, BRANCH_W), BF16)),
        grid=(batch, nc),
        in_specs=[
            pl.BlockSpec((L, BRANCH_W), fwd), pl.BlockSpec((L, 512), fwd),
            pl.BlockSpec((L, LANES), fwd), pl.BlockSpec((nh2, L), fwd_t),
            pl.BlockSpec((L, BRANCH_W), bwd), pl.BlockSpec((L, 512), bwd),
            pl.BlockSpec((L, LANES), bwd), pl.BlockSpec((nh2, L), bwd_t),
            pl.BlockSpec((1, LANES), const), pl.BlockSpec((nh2, 1), const),
            pl.BlockSpec((1, LANES), const), pl.BlockSpec((nh2, 1), const),
            pl.BlockSpec((1, BRANCH_W), const), pl.BlockSpec((2, LANES, BRANCH_W), lambda b, j: (0, 0, 0)),
        ],
        out_specs=(pl.BlockSpec((L, BRANCH_W), fwd), pl.BlockSpec((L, BRANCH_W), bwd)),
        scratch_shapes=[pltpu.VMEM((SSD_GROUPS, SSD_STATE, 512), F32), pltpu.VMEM((SSD_GROUPS, SSD_STATE, 512), F32)],
        compiler_params=_cparams(("arbitrary", "arbitrary")),
        name="ssd_scan",
    )(xs_c, bc_c, small, small_t, xs_c, bc_c, small, small_t, dtb_r, dtb_c, alog_r, alog_c, dskip, expand)


def _swap32(y):
    lane = lax.broadcasted_iota(jnp.int32, y.shape, 1)
    lo = (lane % 64) < 32
    return jnp.where(lo, pltpu.roll(y, 96, 1), pltpu.roll(y, 32, 1))


def _gqa_prep_kernel(q_ref, k_ref, cos_ref, sin_ref, qw_ref, kw_ref, qo_ref, ko_ref, *, scale):
    cos = cos_ref[...]
    sin = sin_ref[...]

    def one(x, w, s):
        x = x.astype(F32)
        ms = jnp.mean(x * x, axis=-1, keepdims=True)
        y = x * lax.rsqrt(ms + EPS) * w
        return ((y * cos + _swap32(y) * sin) * s).astype(BF16)

    for h in range(GQA_HEADS):
        sl = slice(LANES * h, LANES * (h + 1))
        qo_ref[:, sl] = one(q_ref[:, sl], qw_ref[...], scale)
    for h in range(GQA_KV_HEADS):
        sl = slice(LANES * h, LANES * (h + 1))
        ko_ref[:, sl] = one(k_ref[:, sl], kw_ref[...], 1.0)


def _gqa_prep_call(big, cos_t, sin_t, qw, kw, seq_len):
    t = big.shape[0]
    tm = 512
    tps = seq_len // tm
    kern = functools.partial(_gqa_prep_kernel, scale=GQA_HEAD_DIM ** -0.5)
    return pl.pallas_call(
        kern,
        out_shape=(jax.ShapeDtypeStruct((t, BRANCH_W), BF16), jax.ShapeDtypeStruct((t, 256), BF16)),
        grid=(t // tm,),
        in_specs=[
            pl.BlockSpec((tm, BRANCH_W), lambda i: (i, C_GQ // BRANCH_W)),
            pl.BlockSpec((tm, 256), lambda i: (i, C_GK // 256)),
            pl.BlockSpec((tm, LANES), lambda i: (i % tps, 0)),
            pl.BlockSpec((tm, LANES), lambda i: (i % tps, 0)),
            pl.BlockSpec((1, LANES), lambda i: (0, 0)),
            pl.BlockSpec((1, LANES), lambda i: (0, 0)),
        ],
        out_specs=(pl.BlockSpec((tm, BRANCH_W), lambda i: (i, 0)), pl.BlockSpec((tm, 256), lambda i: (i, 0))),
        compiler_params=_cparams(("arbitrary",)),
        name="gqa_prep",
    )(big, big, cos_t, sin_t, qw, kw)


def _na_prep_kernel(q_ref, k_ref, qw_ref, kw_ref, qo_ref, ko_ref, *, scale):
    def one(x, w, s):
        x = x.astype(F32)
        lane = lax.broadcasted_iota(jnp.int32, x.shape, 1)
        lo = lane < NA_HEAD_DIM
        x2 = x * x
        s_lo = jnp.sum(jnp.where(lo, x2, 0.0), axis=-1, keepdims=True)
        s_hi = jnp.sum(jnp.where(lo, 0.0, x2), axis=-1, keepdims=True)
        ms = jnp.where(lo, s_lo, s_hi) * (1.0 / NA_HEAD_DIM)
        return (x * lax.rsqrt(ms + EPS) * w * s).astype(BF16)

    for hp in range(NA_HEADS // 2):
        sl = slice(LANES * hp, LANES * (hp + 1))
        qo_ref[:, sl] = one(q_ref[:, sl], qw_ref[...], scale)
        ko_ref[:, sl] = one(k_ref[:, sl], kw_ref[...], 1.0)


def _na_prep_call(big, qw2, kw2):
    t = big.shape[0]
    tm = 512
    kern = functools.partial(_na_prep_kernel, scale=NA_HEAD_DIM ** -0.5)
    return pl.pallas_call(
        kern,
        out_shape=(jax.ShapeDtypeStruct((t, BRANCH_W), BF16), jax.ShapeDtypeStruct((t, BRANCH_W), BF16)),
        grid=(t // tm,),
        in_specs=[
            pl.BlockSpec((tm, BRANCH_W), lambda i: (i, C_NQ // BRANCH_W)),
            pl.BlockSpec((tm, BRANCH_W), lambda i: (i, C_NK // BRANCH_W)),
            pl.BlockSpec((1, LANES), lambda i: (0, 0)),
            pl.BlockSpec((1, LANES), lambda i: (0, 0)),
        ],
        out_specs=(pl.BlockSpec((tm, BRANCH_W), lambda i: (i, 0)), pl.BlockSpec((tm, BRANCH_W), lambda i: (i, 0))),
        compiler_params=_cparams(("arbitrary",)),
        name="na_prep",
    )(big, big, qw2, kw2)


def _flash_kernel(q_ref, k_ref, v_ref, o_ref, qs, m_scr, l_scr, acc, *, tq, grp):
    ki = pl.program_id(3)

    @pl.when(ki == 0)
    def _():
        for h in range(grp):
            qs[tq * h:tq * (h + 1), :] = q_ref[:, LANES * h:LANES * (h + 1)]
        m_scr[...] = jnp.full(m_scr.shape, -jnp.inf, F32)
        l_scr[...] = jnp.zeros(l_scr.shape, F32)
        acc[...] = jnp.zeros(acc.shape, F32)

    st = lax.dot_general(k_ref[...], qs[...], (((1,), (1,)), ((), ())), preferred_element_type=F32)
    m_old = m_scr[...]
    m_new = jnp.maximum(m_old, jnp.max(st, axis=0, keepdims=True))
    alpha = jnp.exp(m_old - m_new)
    p = jnp.exp(st - m_new)
    l_scr[...] = alpha * l_scr[...] + jnp.sum(p, axis=0, keepdims=True)
    pv = lax.dot_general(v_ref[...], p.astype(BF16), (((0,), (0,)), ((), ())), preferred_element_type=F32)
    acc[...] = alpha * acc[...] + pv
    m_scr[...] = m_new

    @pl.when(ki == pl.num_programs(3) - 1)
    def _():
        o = (acc[...] / l_scr[...]).T
        for h in range(grp):
            o_ref[:, LANES * h:LANES * (h + 1)] = o[tq * h:tq * (h + 1), :].astype(BF16)


def _flash_call(qp, kp, big, batch, seq_len):
    t = qp.shape[0]
    grp = GQA_HEADS // GQA_KV_HEADS
    tq, tk = 256, 512
    nq, nk = seq_len // tq, seq_len // tk
    gw = grp * LANES
    kern = functools.partial(_flash_kernel, tq=tq, grp=grp)
    return pl.pallas_call(
        kern,
        out_shape=jax.ShapeDtypeStruct((t, BRANCH_W), BF16),
        grid=(batch, GQA_KV_HEADS, nq, nk),
        in_specs=[
            pl.BlockSpec((tq, gw), lambda b, g, qi, ki: (b * nq + qi, g)),
            pl.BlockSpec((tk, LANES), lambda b, g, qi, ki: (b * nk + ki, g)),
            pl.BlockSpec((tk, LANES), lambda b, g, qi, ki: (b * nk + ki, C_GV // LANES + g)),
        ],
        out_specs=pl.BlockSpec((tq, gw), lambda b, g, qi, ki: (b * nq + qi, g)),
        scratch_shapes=[pltpu.VMEM((grp * tq, LANES), BF16), pltpu.VMEM((1, grp * tq), F32),
                        pltpu.VMEM((1, grp * tq), F32), pltpu.VMEM((LANES, grp * tq), F32)],
        compiler_params=_cparams(("arbitrary", "arbitrary", "arbitrary", "arbitrary")),
        name="gqa_flash",
    )(qp, kp, big)


NA_ROWS_PER_STEP = 4


def _na_key_rows(r):
    kr = r + NA_WIN_ROWS - 1
    return kr + (kr * GRID_W % LANES) // GRID_W


def _na_kernel(q_ref, k_ref, v_ref, tbl_ref, o_ref, *, rows, R):
    KR = _na_key_rows(R)
    W = GRID_W
    rb = pl.program_id(2)
    r0 = rb * R
    ks = jnp.clip(r0 - NA_WIN_ROWS // 2, 0, rows - KR)
    kstart = pl.multiple_of(ks * W, W)
    kblk = k_ref[pl.ds(kstart, KR * W), :]
    vblk = v_ref[pl.ds(kstart, KR * W), :]
    q = q_ref[...]
    lane = lax.broadcasted_iota(jnp.int32, (W, LANES), 1)
    head0 = lane < NA_HEAD_DIM
    zero = jnp.zeros((W, LANES), BF16)
    parts = []
    for qr in range(R):
        qrow = q[W * qr:W * (qr + 1), :]
        parts.append(jnp.where(head0, qrow, zero))
        parts.append(jnp.where(head0, zero, qrow))
    qcat = jnp.concatenate(parts, axis=0)
    st = lax.dot_general(kblk, qcat, (((1,), (1,)), ((), ())), preferred_element_type=F32)
    blocks = []
    for a in range(KR):
        rowb = []
        for qr in range(R):
            krow = ks + a
            r = r0 + qr
            rs = jnp.clip(r - NA_WIN_ROWS // 2, 0, rows - NA_WIN_ROWS)
            ok = jnp.logical_and(krow >= rs, krow < rs + NA_WIN_ROWS)
            idx = jnp.where(ok, krow - r + (NA_WIN_ROWS - 1), 2 * NA_WIN_ROWS - 1)
            rowb.append(st[W * a:W * (a + 1), LANES * qr:LANES * (qr + 1)] + tbl_ref[idx])
        blocks.append(jnp.concatenate(rowb, axis=1))
    s = jnp.concatenate(blocks, axis=0)
    m = jnp.max(s, axis=0, keepdims=True)
    p = jnp.exp(s - m).astype(BF16)
    vaug = jnp.concatenate([vblk, jnp.ones_like(vblk)], axis=1)
    o = lax.dot_general(p, vaug, (((0,), (0,)), ((), ())), preferred_element_type=F32)
    o = o[:, :LANES] / o[:, LANES:]
    for qr in range(R):
        top = o[2 * W * qr:2 * W * qr + W, :]
        bot = o[2 * W * qr + W:2 * W * (qr + 1), :]
        o_ref[W * qr:W * (qr + 1), :] = jnp.where(head0, top, bot).astype(BF16)


def _na_call(qn, kn, big, tbl, batch, seq_len):
    t = qn.shape[0]
    rows = seq_len // GRID_W
    R = NA_ROWS_PER_STEP
    nrb = rows // R
    kern = functools.partial(_na_kernel, rows=rows, R=R)
    return pl.pallas_call(
        kern,
        out_shape=jax.ShapeDtypeStruct((t, BRANCH_W), BF16),
        grid=(batch, NA_HEADS // 2, nrb),
        in_specs=[
            pl.BlockSpec((R * GRID_W, LANES), lambda b, hp, rb: (b * nrb + rb, hp)),
            pl.BlockSpec((seq_len, LANES), lambda b, hp, rb: (b, hp)),
            pl.BlockSpec((seq_len, LANES), lambda b, hp, rb: (b, C_NV // LANES + hp)),
            pl.BlockSpec((None, 2 * NA_WIN_ROWS, GRID_W, LANES), lambda b, hp, rb: (hp, 0, 0, 0)),
        ],
        out_specs=pl.BlockSpec((R * GRID_W, LANES), lambda b, hp, rb: (b * nrb + rb, hp)),
        compiler_params=_cparams(("arbitrary", "arbitrary", "arbitrary")),
        name="na_attn",
    )(qn, kn, big, tbl)


def _mlstm_direction(q_ref, k_ref, v_ref, sm_ref, smt_ref, gb_r, gb_c, c_scr, n_scr, m_scr, o_ref, d, backward):
    L = CHUNK
    g_c = sm_ref[...] + gb_r[...]
    g_r = smt_ref[...] + gb_c[...]
    lf_c = -_softplus(-g_c)
    lf_r = -_softplus(-g_r)
    m_c = _tri(L, lower=not backward).astype(BF16)
    m_r = _tri(L, lower=backward).astype(BF16)
    bcs_c_all = _dot_exact_lhs(m_c, lf_c)
    bcs_r_all = _dot_exact_rhs(lf_r, m_r)
    valid = _tri(L, lower=not backward)
    for h in range(ML_HEADS):
        ii = 2 * ML_HEADS * d + h
        fi = ii + ML_HEADS
        li_c = g_c[:, S_MG + ii:S_MG + ii + 1]
        li_r = g_r[ii:ii + 1, :]
        bcs_c = bcs_c_all[:, S_MG + fi:S_MG + fi + 1]
        bcs_r = bcs_r_all[fi:fi + 1, :]
        b_end = bcs_c[0:1, :] if backward else bcs_c[L - 1:L, :]
        dm = jnp.where(valid, bcs_c - (bcs_r - li_r), -jnp.inf)
        m_prev = m_scr[h, 0:1, 0:1]
        inter = bcs_c + m_prev
        m_t = jnp.maximum(inter, jnp.max(dm, axis=1, keepdims=True))
        sl = slice(ML_HEAD_DIM * h, ML_HEAD_DIM * (h + 1))
        qh = q_ref[:, sl]
        kh = k_ref[:, sl] * (ML_HEAD_DIM ** -0.5)
        vh = v_ref[:, sl]
        sc = lax.dot_general(qh, kh, (((1,), (1,)), ((), ())), preferred_element_type=F32)
        w_intra = jnp.exp(dm - m_t) * sc
        w_inter = jnp.exp(inter - m_t)
        c_prev = c_scr[h]
        n_prev = n_scr[h, 0:1, :]
        num = (jnp.dot(w_intra.astype(BF16), vh, preferred_element_type=F32)
               + w_inter * jnp.dot(qh, c_prev.astype(BF16), preferred_element_type=F32))
        den = (jnp.sum(w_intra, axis=1, keepdims=True)
               + w_inter * jnp.sum(qh.astype(F32) * n_prev, axis=1, keepdims=True))
        den = jnp.maximum(jnp.abs(den), jnp.exp(-m_t))
        o_ref[:, sl] = (num / den).astype(BF16)
        g_end_r = b_end - bcs_r + li_r
        m_loc = jnp.max(g_end_r, axis=1, keepdims=True)
        w_end_c = jnp.exp(b_end - bcs_c + li_c - m_loc)
        kw = kh.astype(F32) * w_end_c
        c_loc = lax.dot_general(kw.astype(BF16), vh, (((0,), (0,)), ((), ())), preferred_element_type=F32)
        n_loc = jnp.sum(kw, axis=0, keepdims=True)
        m_new = jnp.maximum(b_end + m_prev, m_loc)
        a_old = jnp.exp(b_end + m_prev - m_new)
        a_loc = jnp.exp(m_loc - m_new)
        c_scr[h] = a_old * c_prev + a_loc * c_loc
        n_scr[h] = jnp.broadcast_to(a_old * n_prev + a_loc * n_loc, n_scr.shape[1:])
        m_scr[h] = jnp.broadcast_to(m_new, m_scr.shape[1:])


def _mlstm_kernel(qf, kf, vf, smf, smtf, qb, kb, vb, smb, smtb, gb_r, gb_c, hf_ref, hb_ref,
                  cf, nf, mf, cb, nb, mb):
    @pl.when(pl.program_id(1) == 0)
    def _():
        for r in (cf, nf, mf, cb, nb, mb):
            r[...] = jnp.zeros_like(r)

    _mlstm_direction(qf, kf, vf, smf, smtf, gb_r, gb_c, cf, nf, mf, hf_ref, 0, False)
    _mlstm_direction(qb, kb, vb, smb, smtb, gb_r, gb_c, cb, nb, mb, hb_ref, 1, True)


def _mlstm_call(big, small, small_t, gb_r, gb_c, batch, seq_len):
    t = big.shape[0]
    L = CHUNK
    nc = seq_len // L
    cq, ck, cv = C_MQ // BRANCH_W, C_MK // BRANCH_W, C_MV // BRANCH_W
    f = lambda c: (lambda b, j: (b * nc + j, c))
    r = lambda c: (lambda b, j: (b * nc + (nc - 1 - j), c))
    const = lambda b, j: (0, 0)
    state = [pltpu.VMEM((ML_HEADS, ML_HEAD_DIM, ML_HEAD_DIM), F32), pltpu.VMEM((ML_HEADS, 8, ML_HEAD_DIM), F32),
             pltpu.VMEM((ML_HEADS, 8, LANES), F32)]
    return pl.pallas_call(
        _mlstm_kernel,
        out_shape=(jax.ShapeDtypeStruct((t, BRANCH_W), BF16), jax.ShapeDtypeStruct((t, BRANCH_W), BF16)),
        grid=(batch, nc),
        in_specs=[
            pl.BlockSpec((L, BRANCH_W), f(cq)), pl.BlockSpec((L, BRANCH_W), f(ck)), pl.BlockSpec((L, BRANCH_W), f(cv)),
            pl.BlockSpec((L, LANES), f(0)), pl.BlockSpec((16, L), lambda b, j: (S_MG // 16, b * nc + j)),
            pl.BlockSpec((L, BRANCH_W), r(cq)), pl.BlockSpec((L, BRANCH_W), r(ck)), pl.BlockSpec((L, BRANCH_W), r(cv)),
            pl.BlockSpec((L, LANES), r(0)), pl.BlockSpec((16, L), lambda b, j: (S_MG // 16, b * nc + (nc - 1 - j))),
            pl.BlockSpec((1, LANES), const), pl.BlockSpec((16, 1), const),
        ],
        out_specs=(pl.BlockSpec((L, BRANCH_W), f(0)), pl.BlockSpec((L, BRANCH_W), r(0))),
        scratch_shapes=state + state,
        compiler_params=_cparams(("arbitrary", "arbitrary")),
        name="mlstm_scan",
    )(big, big, big, small, small_t, big, big, big, small, small_t, gb_r, gb_c)


def _merge_kernel(yf_ref, yb_ref, z_ref, og_ref, on_ref, hf_ref, hb_ref, mo_ref, gate_ref, snw_ref, mnw_ref,
                  wb_ref, o_ref):
    y = (yf_ref[...].astype(F32) + yb_ref[...].astype(F32)) * _silu(z_ref[...].astype(F32))
    ms = jnp.mean(y * y, axis=-1, keepdims=True)
    y0 = (y * lax.rsqrt(ms + EPS) * snw_ref[...]).astype(BF16)
    parts = []
    for h in range(ML_HEADS):
        sl = slice(ML_HEAD_DIM * h, ML_HEAD_DIM * (h + 1))
        hh = hf_ref[:, sl].astype(F32) + hb_ref[:, sl].astype(F32)
        ms = jnp.mean(hh * hh, axis=-1, keepdims=True)
        parts.append(hh * lax.rsqrt(ms + EPS) * mnw_ref[:, sl])
    y3 = (jax.nn.sigmoid(mo_ref[...].astype(F32)) * jnp.concatenate(parts, axis=1)).astype(BF16)
    acc = None
    for i, yi in enumerate((y0, og_ref[...], on_ref[...], y3)):
        g = gate_ref[:, D_MODEL * i:D_MODEL * (i + 1)].astype(F32)
        term = g * jnp.dot(yi, wb_ref[i], preferred_element_type=F32)
        acc = term if acc is None else acc + term
    o_ref[...] = acc.astype(BF16)


def _merge_call(yf, yb, og, on, hf, hb, big, snw, mnw, wb):
    t = big.shape[0]
    tm = 256
    row = lambda i: (i, 0)
    const2 = lambda i: (0, 0)
    return pl.pallas_call(
        _merge_kernel,
        out_shape=jax.ShapeDtypeStruct((t, D_MODEL), BF16),
        grid=(t // tm,),
        in_specs=[
            pl.BlockSpec((tm, BRANCH_W), row), pl.BlockSpec((tm, BRANCH_W), row),
            pl.BlockSpec((tm, BRANCH_W), lambda i: (i, C_Z // BRANCH_W)),
            pl.BlockSpec((tm, BRANCH_W), row), pl.BlockSpec((tm, BRANCH_W), row),
            pl.BlockSpec((tm, BRANCH_W), row), pl.BlockSpec((tm, BRANCH_W), row),
            pl.BlockSpec((tm, BRANCH_W), lambda i: (i, C_MO // BRANCH_W)),
            pl.BlockSpec((tm, N_BRANCH * D_MODEL), lambda i: (i, 0)),
            pl.BlockSpec((1, BRANCH_W), const2), pl.BlockSpec((1, BRANCH_W), const2),
            pl.BlockSpec((N_BRANCH, BRANCH_W, D_MODEL), lambda i: (0, 0, 0), pipeline_mode=pl.Buffered(1)),
        ],
        out_specs=pl.BlockSpec((tm, D_MODEL), row),
        compiler_params=_cparams(("arbitrary",)),
        name="merge",
    )(yf, yb, big, og, on, hf, hb, big, big, snw, mnw, wb)


def _resproj_kernel(a_ref, w_ref, x_ref, mod_ref, o_ref, *, mod_row):
    acc = jnp.dot(a_ref[...], w_ref[...], preferred_element_type=F32)
    o_ref[...] = x_ref[...] + mod_ref[mod_row:mod_row + 1, :] * acc


def _resproj_call(a, w, x, mod, mod_row, seq_len, tm, tn):
    t, k = a.shape
    n = w.shape[1]
    tiles_per_seq = seq_len // tm
    kern = functools.partial(_resproj_kernel, mod_row=mod_row)
    return pl.pallas_call(
        kern,
        out_shape=jax.ShapeDtypeStruct((t, n), F32),
        grid=(n // tn, t // tm),
        in_specs=[
            pl.BlockSpec((tm, k), lambda j, i: (i, 0)),
            pl.BlockSpec((k, tn), lambda j, i: (0, j)),
            pl.BlockSpec((tm, tn), lambda j, i: (i, j)),
            pl.BlockSpec((None, 6, tn), lambda j, i: (i // tiles_per_seq, 0, j)),
        ],
        out_specs=pl.BlockSpec((tm, tn), lambda j, i: (i, j)),
        compiler_params=_cparams(("arbitrary", "arbitrary")),
        name="resproj",
    )(a, w, x, mod)


def _ffn_in_kernel(x_ref, mod_ref, nw_ref, wu_ref, wg_ref, o_ref, h_scr, *, tm, rc):
    @pl.when(pl.program_id(1) == 0)
    def _():
        nw = nw_ref[...]
        sh = mod_ref[3:4, :]
        sc = 1.0 + mod_ref[4:5, :]

        def body(r, carry):
            r0 = pl.multiple_of(r * rc, rc)
            h_scr[pl.ds(r0, rc), :] = _norm_mod_rows(x_ref, nw, sc, sh, r0, rc).astype(BF16)
            return carry

        lax.fori_loop(0, tm // rc, body, 0)

    h = h_scr[...]
    up = jnp.dot(h, wu_ref[...], preferred_element_type=F32)
    gate = jnp.dot(h, wg_ref[...], preferred_element_type=F32)
    o_ref[...] = (_silu(gate) * up).astype(BF16)


def _ffn_in_call(x, mod, nw, w_ffn_in, seq_len):
    t = x.shape[0]
    tm, tn, rc = 1024, 512, 128
    nt = FFN_HIDDEN // tn
    tiles_per_seq = seq_len // tm
    kern = functools.partial(_ffn_in_kernel, tm=tm, rc=rc)
    return pl.pallas_call(
        kern,
        out_shape=jax.ShapeDtypeStruct((t, FFN_HIDDEN), BF16),
        grid=(t // tm, nt),
        in_specs=[
            pl.BlockSpec((tm, D_MODEL), lambda i, j: (i, 0)),
            pl.BlockSpec((None, 6, D_MODEL), lambda i, j: (i // tiles_per_seq, 0, 0)),
            pl.BlockSpec((1, D_MODEL), lambda i, j: (0, 0)),
            pl.BlockSpec((D_MODEL, tn), lambda i, j: (0, j)),
            pl.BlockSpec((D_MODEL, tn), lambda i, j: (0, nt + j)),
        ],
        out_specs=pl.BlockSpec((tm, tn), lambda i, j: (i, j)),
        scratch_shapes=[pltpu.VMEM((tm, D_MODEL), BF16)],
        compiler_params=_cparams(("arbitrary", "arbitrary")),
        name="ffn_in",
    )(x, mod, nw, w_ffn_in, w_ffn_in)


def _rope_tables(seq_len):
    pos = np.arange(seq_len)
    rows, cols = pos // GRID_W, pos % GRID_W
    d = GQA_HEAD_DIM // 2
    inv = np.asarray(ROPE_THETA ** (-np.arange(0, d, 2, dtype=np.float32) / d), dtype=np.float32)

    def cs(p):
        ang = p[:, None].astype(np.float32) * inv[None]
        return np.cos(ang), np.sin(ang)

    cr, sr = cs(rows)
    cc, sc = cs(cols)
    cos = np.concatenate([cr, cr, cc, cc], axis=1).astype(np.float32)
    sin = np.concatenate([-sr, sr, -sc, sc], axis=1).astype(np.float32)
    return jnp.asarray(cos), jnp.asarray(sin)


def _na_bias_table(rpb):
    c = np.arange(GRID_W)[:, None]
    j = np.arange(GRID_W)[None, :]
    start = np.clip(j - NA_WIN_COLS // 2, 0, GRID_W - NA_WIN_COLS)
    ok = (c >= start) & (c < start + NA_WIN_COLS)
    dc = np.clip(c - j + (NA_WIN_COLS - 1), 0, 2 * NA_WIN_COLS - 2)
    t = rpb[:, :, dc]
    t = jnp.where(jnp.asarray(ok)[None, None], t, NEG)
    t = t.reshape(NA_HEADS // 2, 2, 2 * NA_WIN_ROWS - 1, GRID_W, GRID_W)
    t = t.transpose(0, 2, 3, 1, 4).reshape(NA_HEADS // 2, 2 * NA_WIN_ROWS - 1, GRID_W, LANES)
    pad = jnp.full((NA_HEADS // 2, 1, GRID_W, LANES), NEG, F32)
    return jnp.concatenate([t, pad], axis=1)


def _lane_row(v, offset):
    return jnp.zeros((1, LANES), F32).at[0, offset:offset + v.shape[0]].set(v)


def _layer_weights(l, w_in, w_bgate, b_bgate):
    wi = w_in[l]
    o = np.cumsum((0, 1024, 1536, 32, 1024, 256, 256, 1024, 1024, 1024, 1024, 1024, 1024, 1024, 16))
    z, xbc, dt, gq, gk, gv, nq, nk, nv, mq, mk, mv, mo, mg = [wi[:, o[i]:o[i + 1]] for i in range(14)]
    w_big = jnp.concatenate([w_bgate[l], z, xbc[:, :1024], gq, nq, nk, nv, mq, mk, mv, mo, xbc[:, 1024:], gk, gv],
                            axis=1).astype(BF16)
    b_big = jnp.concatenate([b_bgate[l], jnp.zeros((BIG_COLS - N_BRANCH * D_MODEL,), F32)])[None, :]
    ws = jnp.concatenate([dt, mg, jnp.zeros((D_MODEL, LANES - 48), F32)], axis=1)
    ws_hi = ws.astype(BF16)
    ws_lo = (ws - ws_hi.astype(F32)).astype(BF16)
    return w_big, b_big, ws_hi, ws_lo


def _trunk_group(x, mods, seq_len, batch, consts, layers):
    for l in range(DEPTH):
        lw = layers[l]
        mod = mods[l]
        big, small = _inproj_call(x, mod, lw["norm1"], lw["w_big"], lw["b_big"], lw["ws_hi"], lw["ws_lo"], seq_len)
        small_t = small.T
        xs_c, bc_c = _conv_call(big, lw["conv_w1"], lw["conv_b1"], lw["conv_w2"], lw["conv_b2"], seq_len)
        yf, yb = _ssd_call(xs_c, bc_c, small, small_t, lw["dtb_r"], lw["dtb_c"], lw["alog_r"], lw["alog_c"],
                           lw["dskip"], consts["expand"], batch, seq_len)
        qp, kp = _gqa_prep_call(big, consts["cos"][seq_len], consts["sin"][seq_len], lw["gqa_qw"], lw["gqa_kw"],
                                seq_len)
        og = _flash_call(qp, kp, big, batch, seq_len)
        qn, kn = _na_prep_call(big, lw["na_qw"], lw["na_kw"])
        on = _na_call(qn, kn, big, lw["na_tbl"], batch, seq_len)
        hf, hb = _mlstm_call(big, small, small_t, lw["ml_gb_r"], lw["ml_gb_c"], batch, seq_len)
        merged = _merge_call(yf, yb, og, on, hf, hb, big, lw["ssd_nw"], lw["ml_nw"], lw["wb"])
        x = _resproj_call(merged, lw["w_o"], x, mod, 2, seq_len, 512, 1024)
        act = _ffn_in_call(x, mod, lw["norm2"], lw["w_ffn_in"], seq_len)
        x = _resproj_call(act, lw["w_ffn_out"], x, mod, 5, seq_len, 512, 1024)
    return x


def kernel(x_prompt, x_sample, c_prompt, c_sample, w_ada, b_ada, norm1_w, w_in, w_bgate, b_bgate, ssd_conv_w,
           ssd_conv_b, ssd_dt_bias, ssd_a_log, ssd_d, ssd_norm_w, gqa_q_norm, gqa_k_norm, na_q_norm, na_k_norm,
           na_rpb, ml_gate_bias, ml_out_norm, w_branch_out, w_o, norm2_w, w_ffn_in, w_ffn_out):
    bp, sp, _ = x_prompt.shape
    bs, ss, _ = x_sample.shape
    nseq = bp + bs
    rows = -(-nseq // 16) * 16
    c_all = jnp.concatenate([c_prompt, c_sample, jnp.zeros((rows - nseq, D_MODEL), F32)], axis=0)
    mods = _ada_call(c_all, w_ada.astype(BF16), b_ada.reshape(DEPTH, 1, 6 * D_MODEL))
    mods = mods.reshape(DEPTH, rows, 6, D_MODEL)

    expand = np.zeros((2, LANES, BRANCH_W), np.float32)
    for d in range(2):
        for h in range(SSD_HEADS):
            expand[d, SSD_HEADS * d + h, SSD_HEAD_DIM * h:SSD_HEAD_DIM * (h + 1)] = 1.0
    consts = {"expand": jnp.asarray(expand, BF16), "cos": {}, "sin": {}}
    for s in {sp, ss}:
        consts["cos"][s], consts["sin"][s] = _rope_tables(s)

    layers = []
    for l in range(DEPTH):
        w_big, b_big, ws_hi, ws_lo = _layer_weights(l, w_in, w_bgate, b_bgate)
        layers.append({
            "norm1": norm1_w[l][None, :], "norm2": norm2_w[l][None, :],
            "w_big": w_big, "b_big": b_big, "ws_hi": ws_hi, "ws_lo": ws_lo,
            "conv_w1": ssd_conv_w[l][:, :BRANCH_W], "conv_b1": ssd_conv_b[l][None, :BRANCH_W],
            "conv_w2": ssd_conv_w[l][:, BRANCH_W:], "conv_b2": ssd_conv_b[l][None, BRANCH_W:],
            "dtb_r": _lane_row(ssd_dt_bias[l].reshape(-1), S_DT), "dtb_c": ssd_dt_bias[l].reshape(2 * SSD_HEADS, 1),
            "alog_r": _lane_row(ssd_a_log[l].reshape(-1), S_DT), "alog_c": ssd_a_log[l].reshape(2 * SSD_HEADS, 1),
            "dskip": jnp.repeat(ssd_d[l], SSD_HEAD_DIM)[None, :],
            "ssd_nw": ssd_norm_w[l][None, :],
            "gqa_qw": gqa_q_norm[l][None, :], "gqa_kw": gqa_k_norm[l][None, :],
            "na_qw": jnp.tile(na_q_norm[l], 2)[None, :], "na_kw": jnp.tile(na_k_norm[l], 2)[None, :],
            "na_tbl": _na_bias_table(na_rpb[l]),
            "ml_gb_r": _lane_row(ml_gate_bias[l].reshape(-1), S_MG), "ml_gb_c": ml_gate_bias[l].reshape(16, 1),
            "ml_nw": ml_out_norm[l][None, :],
            "wb": w_branch_out[l].astype(BF16), "w_o": w_o[l].astype(BF16),
            "w_ffn_in": w_ffn_in[l].astype(BF16), "w_ffn_out": w_ffn_out[l].astype(BF16),
        })

    y_p = _trunk_group(x_prompt.reshape(bp * sp, D_MODEL), mods[:, :bp], sp, bp, consts, layers)
    y_s = _trunk_group(x_sample.reshape(bs * ss, D_MODEL), mods[:, bp:bp + bs], ss, bs, consts, layers)
    return y_p.reshape(bp, sp, D_MODEL), y_s.reshape(bs, ss, D_MODEL)
```

```python
import functools
import math

import numpy as np
import jax
import jax.numpy as jnp
from jax import lax
from jax.experimental import pallas as pl
from jax.experimental.pallas import tpu as pltpu

F32 = jnp.float32
BF16 = jnp.bfloat16

D_MODEL = 2048
DEPTH = 4
GRID_W = 64
EPS = 1e-6
N_BRANCH = 4
BRANCH_W = 1024
SSD_HEAD_DIM = 64
SSD_HEADS = 16
SSD_GROUPS = 2
SSD_STATE = 128
SSD_CONV = 5
CHUNK = 128
GQA_HEAD_DIM = 128
GQA_HEADS = 8
GQA_KV_HEADS = 2
ROPE_THETA = 10000.0
NA_HEAD_DIM = 64
NA_HEADS = 16
NA_WIN_ROWS = 8
NA_WIN_COLS = 16
ML_HEADS = 4
ML_HEAD_DIM = 256
FFN_HIDDEN = 5632

VMEM_LIMIT_BYTES = 56 * 1024 * 1024
LANES = 128

C_GATE = 0
C_Z = 8192
C_XS = 9216
C_GQ = 10240
C_NQ = 11264
C_NK = 12288
C_NV = 13312
C_MQ = 14336
C_MK = 15360
C_MV = 16384
C_MO = 17408
C_BM = 18432
C_GK = 18944
C_GV = 19200
BIG_COLS = 19456
S_DT = 0
S_MG = 32
NEG = -1e30


def _cparams(sem):
    return pltpu.CompilerParams(dimension_semantics=sem, vmem_limit_bytes=VMEM_LIMIT_BYTES)


def _silu(x):
    return x * jax.nn.sigmoid(x)


def _softplus(x):
    return jnp.maximum(x, 0.0) + jnp.log1p(jnp.exp(-jnp.abs(x)))


def _split3(x):
    x1 = x.astype(BF16)
    r1 = x - x1.astype(F32)
    x2 = r1.astype(BF16)
    x3 = (r1 - x2.astype(F32)).astype(BF16)
    return x1, x2, x3


def _dot_exact_rhs(x, m):
    a, b, c = _split3(x)
    d = lambda t: jnp.dot(t, m, preferred_element_type=F32)
    return d(a) + d(b) + d(c)


def _dot_exact_lhs(m, x):
    a, b, c = _split3(x)
    d = lambda t: jnp.dot(m, t, preferred_element_type=F32)
    return d(a) + d(b) + d(c)


def _tri(n, lower):
    r = lax.broadcasted_iota(jnp.int32, (n, n), 0)
    c = lax.broadcasted_iota(jnp.int32, (n, n), 1)
    return (c <= r) if lower else (c >= r)


def _ada_kernel(c_ref, w_ref, b_ref, o_ref):
    a = _silu(c_ref[...]).astype(BF16)
    o_ref[...] = jnp.dot(a, w_ref[...], preferred_element_type=F32) + b_ref[...]


def _ada_call(c_pad, w_ada, b_ada):
    rows = c_pad.shape[0]
    tn = 2048
    nt = w_ada.shape[-1] // tn
    return pl.pallas_call(
        _ada_kernel,
        out_shape=jax.ShapeDtypeStruct((DEPTH, rows, w_ada.shape[-1]), F32),
        grid=(DEPTH, nt),
        in_specs=[
            pl.BlockSpec((rows, D_MODEL), lambda l, j: (0, 0)),
            pl.BlockSpec((None, D_MODEL, tn), lambda l, j: (l, 0, j)),
            pl.BlockSpec((None, 1, tn), lambda l, j: (l, 0, j)),
        ],
        out_specs=pl.BlockSpec((None, rows, tn), lambda l, j: (l, 0, j)),
        compiler_params=_cparams(("arbitrary", "arbitrary")),
        name="ada_mod",
    )(c_pad, w_ada, b_ada)


def _norm_mod_rows(x_ref, nw, sc, sh, r0, rc):
    xf = x_ref[pl.ds(r0, rc), :]
    ms = jnp.mean(xf * xf, axis=-1, keepdims=True)
    return xf * lax.rsqrt(ms + EPS) * nw * sc + sh


def _inproj_kernel(x_ref, mod_ref, nw_ref, w_ref, b_ref, wsh_ref, wsl_ref, big_ref, small_ref, h_scr,
                   *, tm, rc, n_gate_tiles):
    j = pl.program_id(1)

    @pl.when(j == 0)
    def _():
        nw = nw_ref[...]
        sh = mod_ref[0:1, :]
        sc = 1.0 + mod_ref[1:2, :]

        def body(r, carry):
            r0 = pl.multiple_of(r * rc, rc)
            hf = _norm_mod_rows(x_ref, nw, sc, sh, r0, rc)
            hb = hf.astype(BF16)
            h_scr[pl.ds(r0, rc), :] = hb
            lo = (hf - hb.astype(F32)).astype(BF16)
            wh = wsh_ref[...]
            sm = (jnp.dot(hb, wh, preferred_element_type=F32)
                  + jnp.dot(lo, wh, preferred_element_type=F32)
                  + jnp.dot(hb, wsl_ref[...], preferred_element_type=F32))
            small_ref[pl.ds(r0, rc), :] = sm
            return carry

        lax.fori_loop(0, tm // rc, body, 0)

    acc = jnp.dot(h_scr[...], w_ref[...], preferred_element_type=F32) + b_ref[...]
    big_ref[...] = jnp.where(j < n_gate_tiles, jax.nn.sigmoid(acc), acc).astype(BF16)


def _inproj_call(x, mod, nw, w_big, b_big, ws_hi, ws_lo, seq_len):
    t = x.shape[0]
    tm, tn, rc = 1024, 1024, 128
    tiles_per_seq = seq_len // tm
    kern = functools.partial(_inproj_kernel, tm=tm, rc=rc, n_gate_tiles=(N_BRANCH * D_MODEL) // tn)
    return pl.pallas_call(
        kern,
        out_shape=(jax.ShapeDtypeStruct((t, BIG_COLS), BF16), jax.ShapeDtypeStruct((t, LANES), F32)),
        grid=(t // tm, BIG_COLS // tn),
        in_specs=[
            pl.BlockSpec((tm, D_MODEL), lambda i, j: (i, 0)),
            pl.BlockSpec((None, 6, D_MODEL), lambda i, j: (i // tiles_per_seq, 0, 0)),
            pl.BlockSpec((1, D_MODEL), lambda i, j: (0, 0)),
            pl.BlockSpec((D_MODEL, tn), lambda i, j: (0, j)),
            pl.BlockSpec((1, tn), lambda i, j: (0, j)),
            pl.BlockSpec((D_MODEL, LANES), lambda i, j: (0, 0)),
            pl.BlockSpec((D_MODEL, LANES), lambda i, j: (0, 0)),
        ],
        out_specs=(pl.BlockSpec((tm, tn), lambda i, j: (i, j)),
                   pl.BlockSpec((tm, LANES), lambda i, j: (i, 0))),
        scratch_shapes=[pltpu.VMEM((tm, D_MODEL), BF16)],
        compiler_params=_cparams(("arbitrary", "arbitrary")),
        name="inproj",
    )(x, mod, nw, w_big, b_big, ws_hi, ws_lo)


HALO = 16


def _conv_kernel(xs_ref, xsp_ref, xsn_ref, bc_ref, bcp_ref, bcn_ref, w1_ref, b1_ref, w2_ref, b2_ref,
                 oxs_ref, obc_ref, ext1, ext2, *, tm, rc, tiles_per_seq):
    i = pl.program_id(0)
    first = (i % tiles_per_seq) == 0
    last = (i % tiles_per_seq) == (tiles_per_seq - 1)
    pad = SSD_CONV // 2
    for main, prv, nxt, w_ref, b_ref, out, ext in (
            (xs_ref, xsp_ref, xsn_ref, w1_ref, b1_ref, oxs_ref, ext1),
            (bc_ref, bcp_ref, bcn_ref, w2_ref, b2_ref, obc_ref, ext2)):
        ext[HALO:HALO + tm, :] = main[...].astype(F32)
        ext[0:HALO, :] = jnp.where(first, 0.0, prv[...].astype(F32))
        ext[HALO + tm:2 * HALO + tm, :] = jnp.where(last, 0.0, nxt[...].astype(F32))
        for c in range(tm // rc):
            acc = jnp.zeros((rc, ext.shape[1]), F32) + b_ref[...]
            for k in range(SSD_CONV):
                s = HALO + c * rc + k - pad
                acc = acc + w_ref[k:k + 1, :] * ext[s:s + rc, :]
            out[c * rc:(c + 1) * rc, :] = _silu(acc).astype(BF16)


def _conv_call(big, w1, b1, w2, b2, seq_len):
    t = big.shape[0]
    tm, rc = 512, 64
    tiles_per_seq = seq_len // tm
    nb = t // HALO
    hb = tm // HALO
    kern = functools.partial(_conv_kernel, tm=tm, rc=rc, tiles_per_seq=tiles_per_seq)
    prev_map = lambda cb: (lambda i: (jnp.maximum(i * hb - 1, 0), cb))
    next_map = lambda cb: (lambda i: (jnp.minimum((i + 1) * hb, nb - 1), cb))
    cxs = C_XS // BRANCH_W
    cbc = C_BM // 512
    return pl.pallas_call(
        kern,
        out_shape=(jax.ShapeDtypeStruct((t, BRANCH_W), BF16), jax.ShapeDtypeStruct((t, 512), BF16)),
        grid=(t // tm,),
        in_specs=[
            pl.BlockSpec((tm, BRANCH_W), lambda i: (i, cxs)),
            pl.BlockSpec((HALO, BRANCH_W), prev_map(cxs)),
            pl.BlockSpec((HALO, BRANCH_W), next_map(cxs)),
            pl.BlockSpec((tm, 512), lambda i: (i, cbc)),
            pl.BlockSpec((HALO, 512), prev_map(cbc)),
            pl.BlockSpec((HALO, 512), next_map(cbc)),
            pl.BlockSpec((SSD_CONV, BRANCH_W), lambda i: (0, 0)),
            pl.BlockSpec((1, BRANCH_W), lambda i: (0, 0)),
            pl.BlockSpec((SSD_CONV, 512), lambda i: (0, 0)),
            pl.BlockSpec((1, 512), lambda i: (0, 0)),
        ],
        out_specs=(pl.BlockSpec((tm, BRANCH_W), lambda i: (i, 0)),
                   pl.BlockSpec((tm, 512), lambda i: (i, 0))),
        scratch_shapes=[pltpu.VMEM((tm + 2 * HALO, BRANCH_W), F32), pltpu.VMEM((tm + 2 * HALO, 512), F32)],
        compiler_params=_cparams(("arbitrary",)),
        name="ssd_conv",
    )(big, big, big, big, big, big, w1, b1, w2, b2)


def _ssd_direction(xs_ref, bc_ref, sm_ref, smt_ref, dtb_r, dtb_c, alog_r, alog_c, exp_ref, st_ref, d, backward):
    L = CHUNK
    hs = slice(SSD_HEADS * d, SSD_HEADS * (d + 1))
    dt_c = _softplus(sm_ref[...] + dtb_r[...])
    da_c = dt_c * (-jnp.exp(alog_r[...]))
    dt_r = _softplus(smt_ref[hs, :] + dtb_c[hs, :])
    da_r = dt_r * (-jnp.exp(alog_c[hs, :]))
    m_c = _tri(L, lower=not backward).astype(BF16)
    m_r = _tri(L, lower=backward).astype(BF16)
    acs_c = _dot_exact_lhs(m_c, da_c)
    acs_r = _dot_exact_rhs(da_r, m_r)
    valid = _tri(L, lower=not backward)
    acs_last = acs_c[0:1, :] if backward else acs_c[L - 1:L, :]
    w_c = dt_c * jnp.exp(acs_last - acs_c)
    e_c = jnp.exp(acs_c)
    expand = exp_ref[d]

    def widen(v):
        hi = v.astype(BF16)
        lo = (v - hi.astype(F32)).astype(BF16)
        return (jnp.dot(hi, expand, preferred_element_type=F32)
                + jnp.dot(lo, expand, preferred_element_type=F32))

    w_wide = widen(w_c)
    e_wide = widen(e_c)
    cd_wide = e_wide[0:1, :] if backward else e_wide[L - 1:L, :]

    xs = xs_ref[...]
    xs_f = xs.astype(F32)
    lane = lax.broadcasted_iota(jnp.int32, (L, LANES), 1)
    gw = BRANCH_W // SSD_GROUPS
    hpg = SSD_HEADS // SSD_GROUPS
    ys = []
    for g in range(SSD_GROUPS):
        bm = bc_ref[:, SSD_STATE * g:SSD_STATE * (g + 1)]
        cm = bc_ref[:, SSD_GROUPS * SSD_STATE + SSD_STATE * g:SSD_GROUPS * SSD_STATE + SSD_STATE * (g + 1)]
        cb = lax.dot_general(cm, bm, (((1,), (1,)), ((), ())), preferred_element_type=F32)
        state = st_ref[g]
        y_off = jnp.dot(cm, state.astype(BF16), preferred_element_type=F32)
        cols = []
        for hp in range(hpg // 2):
            x_pair = xs[:, gw * g + LANES * hp:gw * g + LANES * (hp + 1)]
            acc = None
            for sub in range(2):
                h = hpg * g + 2 * hp + sub
                hc = SSD_HEADS * d + h
                dm = acs_c[:, hc:hc + 1] - acs_r[h:h + 1, :]
                decay = jnp.exp(jnp.where(valid, dm, -jnp.inf))
                wmat = (cb * decay * dt_r[h:h + 1, :]).astype(BF16)
                keep = (lane < SSD_HEAD_DIM) if sub == 0 else (lane >= SSD_HEAD_DIM)
                xh = jnp.where(keep, x_pair, jnp.zeros_like(x_pair))
                part = jnp.dot(wmat, xh, preferred_element_type=F32)
                acc = part if acc is None else acc + part
            cols.append(acc)
        y_diag = jnp.concatenate(cols, axis=1)
        ys.append(y_diag + y_off * e_wide[:, gw * g:gw * (g + 1)])
        xw = (xs_f[:, gw * g:gw * (g + 1)] * w_wide[:, gw * g:gw * (g + 1)]).astype(BF16)
        new = lax.dot_general(bm, xw, (((0,), (0,)), ((), ())), preferred_element_type=F32)
        st_ref[g] = cd_wide[:, gw * g:gw * (g + 1)] * state + new
    return jnp.concatenate(ys, axis=1), xs_f


def _ssd_kernel(xsf_ref, bcf_ref, smf_ref, smtf_ref, xsb_ref, bcb_ref, smb_ref, smtb_ref,
                dtb_r, dtb_c, alog_r, alog_c, dskip_ref, exp_ref, yf_ref, yb_ref, stf, stb):
    @pl.when(pl.program_id(1) == 0)
    def _():
        stf[...] = jnp.zeros_like(stf)
        stb[...] = jnp.zeros_like(stb)

    y, xs_f = _ssd_direction(xsf_ref, bcf_ref, smf_ref, smtf_ref, dtb_r, dtb_c, alog_r, alog_c, exp_ref, stf,
                             0, False)
    yf_ref[...] = (y + dskip_ref[...] * xs_f).astype(BF16)
    y, _ = _ssd_direction(xsb_ref, bcb_ref, smb_ref, smtb_ref, dtb_r, dtb_c, alog_r, alog_c, exp_ref, stb,
                          1, True)
    yb_ref[...] = y.astype(BF16)


def _ssd_call(xs_c, bc_c, small, small_t, dtb_r, dtb_c, alog_r, alog_c, dskip, expand, batch, seq_len):
    t = xs_c.shape[0]
    L = CHUNK
    nc = seq_len // L
    fwd = lambda b, j: (b * nc + j, 0)
    bwd = lambda b, j: (b * nc + (nc - 1 - j), 0)
    fwd_t = lambda b, j: (0, b * nc + j)
    bwd_t = lambda b, j: (0, b * nc + (nc - 1 - j))
    const = lambda b, j: (0, 0)
    nh2 = 2 * SSD_HEADS
    return pl.pallas_call(
        _ssd_kernel,
        out_shape=(jax.ShapeDtypeStruct((t, BRANCH_W), BF16), jax.ShapeDtypeStruct((t, BRANCH_W), BF16)),
        grid=(batch, nc),
        in_specs=[
            pl.BlockSpec((L, BRANCH_W), fwd), pl.BlockSpec((L, 512), fwd),
            pl.BlockSpec((L, LANES), fwd), pl.BlockSpec((nh2, L), fwd_t),
            pl.BlockSpec((L, BRANCH_W), bwd), pl.BlockSpec((L, 512), bwd),
            pl.BlockSpec((L, LANES), bwd), pl.BlockSpec((nh2, L), bwd_t),
            pl.BlockSpec((1, LANES), const), pl.BlockSpec((nh2, 1), const),
            pl.BlockSpec((1, LANES), const), pl.BlockSpec((nh2, 1), const),
            pl.BlockSpec((1, BRANCH_W), const), pl.BlockSpec((2, LANES, BRANCH_W), lambda b, j: (0, 0, 0)),
        ],
        out_specs=(pl.BlockSpec((L, BRANCH_W), fwd), pl.BlockSpec((L, BRANCH_W), bwd)),
        scratch_shapes=[pltpu.VMEM((SSD_GROUPS, SSD_STATE, 512), F32), pltpu.VMEM((SSD_GROUPS, SSD_STATE, 512), F32)],
        compiler_params=_cparams(("arbitrary", "arbitrary")),
        name="ssd_scan",
    )(xs_c, bc_c, small, small_t, xs_c, bc_c, small, small_t, dtb_r, dtb_c, alog_r, alog_c, dskip, expand)


def _swap32(y):
    lane = lax.broadcasted_iota(jnp.int32, y.shape, 1)
    lo = (lane % 64) < 32
    return jnp.where(lo, pltpu.roll(y, 96, 1), pltpu.roll(y, 32, 1))


def _gqa_prep_kernel(q_ref, k_ref, v_ref, cos_ref, sin_ref, qw_ref, kw_ref, qo_ref, ko_ref, vto_ref, *, scale):
    cos = cos_ref[...]
    sin = sin_ref[...]

    def one(x, w, s):
        x = x.astype(F32)
        ms = jnp.mean(x * x, axis=-1, keepdims=True)
        y = x * lax.rsqrt(ms + EPS) * w
        return ((y * cos + _swap32(y) * sin) * s).astype(BF16)

    for h in range(GQA_HEADS):
        sl = slice(LANES * h, LANES * (h + 1))
        qo_ref[:, sl] = one(q_ref[:, sl], qw_ref[...], scale)
    for h in range(GQA_KV_HEADS):
        sl = slice(LANES * h, LANES * (h + 1))
        ko_ref[:, sl] = one(k_ref[:, sl], kw_ref[...], 1.0)
        vto_ref[sl, :] = v_ref[:, sl].astype(F32).T.astype(BF16)


def _gqa_prep_call(big, cos_t, sin_t, qw, kw, seq_len):
    t = big.shape[0]
    tm = 512
    tps = seq_len // tm
    kern = functools.partial(_gqa_prep_kernel, scale=GQA_HEAD_DIM ** -0.5 * math.log2(math.e))
    kvw = GQA_KV_HEADS * GQA_HEAD_DIM
    return pl.pallas_call(
        kern,
        out_shape=(jax.ShapeDtypeStruct((t, BRANCH_W), BF16), jax.ShapeDtypeStruct((t, kvw), BF16),
                   jax.ShapeDtypeStruct((kvw, t), BF16)),
        grid=(t // tm,),
        in_specs=[
            pl.BlockSpec((tm, BRANCH_W), lambda i: (i, C_GQ // BRANCH_W)),
            pl.BlockSpec((tm, kvw), lambda i: (i, C_GK // kvw)),
            pl.BlockSpec((tm, kvw), lambda i: (i, C_GV // kvw)),
            pl.BlockSpec((tm, LANES), lambda i: (i % tps, 0)),
            pl.BlockSpec((tm, LANES), lambda i: (i % tps, 0)),
            pl.BlockSpec((1, LANES), lambda i: (0, 0)),
            pl.BlockSpec((1, LANES), lambda i: (0, 0)),
        ],
        out_specs=(pl.BlockSpec((tm, BRANCH_W), lambda i: (i, 0)), pl.BlockSpec((tm, kvw), lambda i: (i, 0)),
                   pl.BlockSpec((kvw, tm), lambda i: (0, i))),
        compiler_params=_cparams(("arbitrary",)),
        name="gqa_prep",
    )(big, big, big, cos_t, sin_t, qw, kw)


def _na_prep_kernel(q_ref, k_ref, qw_ref, kw_ref, qo_ref, ko_ref, *, scale):
    def one(x, w, s):
        x = x.astype(F32)
        lane = lax.broadcasted_iota(jnp.int32, x.shape, 1)
        lo = lane < NA_HEAD_DIM
        x2 = x * x
        s_lo = jnp.sum(jnp.where(lo, x2, 0.0), axis=-1, keepdims=True)
        s_hi = jnp.sum(jnp.where(lo, 0.0, x2), axis=-1, keepdims=True)
        ms = jnp.where(lo, s_lo, s_hi) * (1.0 / NA_HEAD_DIM)
        return (x * lax.rsqrt(ms + EPS) * w * s).astype(BF16)

    for hp in range(NA_HEADS // 2):
        sl = slice(LANES * hp, LANES * (hp + 1))
        qo_ref[:, sl] = one(q_ref[:, sl], qw_ref[...], scale)
        ko_ref[:, sl] = one(k_ref[:, sl], kw_ref[...], 1.0)


def _na_prep_call(big, qw2, kw2):
    t = big.shape[0]
    tm = 512
    kern = functools.partial(_na_prep_kernel, scale=NA_HEAD_DIM ** -0.5)
    return pl.pallas_call(
        kern,
        out_shape=(jax.ShapeDtypeStruct((t, BRANCH_W), BF16), jax.ShapeDtypeStruct((t, BRANCH_W), BF16)),
        grid=(t // tm,),
        in_specs=[
            pl.BlockSpec((tm, BRANCH_W), lambda i: (i, C_NQ // BRANCH_W)),
            pl.BlockSpec((tm, BRANCH_W), lambda i: (i, C_NK // BRANCH_W)),
            pl.BlockSpec((1, LANES), lambda i: (0, 0)),
            pl.BlockSpec((1, LANES), lambda i: (0, 0)),
        ],
        out_specs=(pl.BlockSpec((tm, BRANCH_W), lambda i: (i, 0)), pl.BlockSpec((tm, BRANCH_W), lambda i: (i, 0))),
        compiler_params=_cparams(("arbitrary",)),
        name="na_prep",
    )(big, big, qw2, kw2)


def _flash_kernel(q_ref, k_ref, vt_ref, o_ref, m_scr, l_scr, acc, *, tq, tk, grp, nkb, unroll):
    m_scr[...] = jnp.full(m_scr.shape, -jnp.inf, F32)
    l_scr[...] = jnp.zeros(l_scr.shape, F32)
    acc[...] = jnp.zeros(acc.shape, F32)

    def body(it, carry):
        units = [(j, h) for j in range(unroll) for h in range(grp)]

        def kv_start(j):
            return pl.multiple_of((it * unroll + j) * tk, tk)

        def scores(u):
            j, h = units[u]
            kblk = k_ref[pl.ds(kv_start(j), tk), :]
            qh = q_ref[:, LANES * h:LANES * (h + 1)]
            return lax.dot_general(kblk, qh, (((1,), (1,)), ((), ())), preferred_element_type=F32)

        def softmax(u, st):
            h = units[u][1]
            m_old = m_scr[h]
            m_new = jnp.maximum(m_old, jnp.max(st, axis=0, keepdims=True))
            alpha = jnp.exp2(m_old - m_new)
            p = jnp.exp2(st - m_new)
            l_scr[h] = alpha * l_scr[h] + jnp.sum(p, axis=0, keepdims=True)
            m_scr[h] = m_new
            return alpha, p.astype(BF16)

        def accumulate(u, alpha, p):
            j, h = units[u]
            vt = vt_ref[:, pl.ds(kv_start(j), tk)]
            acc[h] = alpha * acc[h] + jnp.dot(vt, p, preferred_element_type=F32)

        n = len(units)
        st = {u: scores(u) for u in range(min(2, n))}
        pending = None
        for u in range(n):
            cur = softmax(u, st.pop(u))
            if u + 2 < n:
                st[u + 2] = scores(u + 2)
            if pending is not None:
                accumulate(u - 1, *pending)
            pending = cur
        accumulate(n - 1, *pending)
        return carry

    lax.fori_loop(0, nkb // unroll, body, 0)
    for h in range(grp):
        o = (acc[h] / l_scr[h]).T
        o_ref[:, LANES * h:LANES * (h + 1)] = o.astype(BF16)


def _flash_call(qp, kp, vt, batch, seq_len):
    t = qp.shape[0]
    grp = GQA_HEADS // GQA_KV_HEADS
    tq, tk = 256, 512
    nq = seq_len // tq
    gw = grp * LANES
    nkb = seq_len // tk
    unroll = math.gcd(nkb, 4)
    kern = functools.partial(_flash_kernel, tq=tq, tk=tk, grp=grp, nkb=nkb, unroll=unroll)
    return pl.pallas_call(
        kern,
        out_shape=jax.ShapeDtypeStruct((t, BRANCH_W), BF16),
        grid=(batch, GQA_KV_HEADS, nq),
        in_specs=[
            pl.BlockSpec((tq, gw), lambda b, g, qi: (b * nq + qi, g)),
            pl.BlockSpec((seq_len, LANES), lambda b, g, qi: (b, g)),
            pl.BlockSpec((LANES, seq_len), lambda b, g, qi: (g, b)),
        ],
        out_specs=pl.BlockSpec((tq, gw), lambda b, g, qi: (b * nq + qi, g)),
        scratch_shapes=[pltpu.VMEM((grp, 1, tq), F32), pltpu.VMEM((grp, 1, tq), F32),
                        pltpu.VMEM((grp, LANES, tq), F32)],
        compiler_params=_cparams(("arbitrary", "arbitrary", "arbitrary")),
        name="gqa_flash",
    )(qp, kp, vt)


NA_ROWS_PER_STEP = 4


def _na_key_rows(r):
    kr = r + NA_WIN_ROWS - 1
    return kr + (kr * GRID_W % LANES) // GRID_W


NA_UNITS_PER_STEP = 4


def _na_kernel(q_ref, k_ref, v_ref, tbl_ref, o_ref, *, rows, R, U):
    KR = _na_key_rows(R)
    W = GRID_W
    lane = lax.broadcasted_iota(jnp.int32, (W, LANES), 1)
    head0 = lane < NA_HEAD_DIM
    zero = jnp.zeros((W, LANES), BF16)

    def row0(u):
        return (pl.program_id(2) * U + u) * R

    def key_row0(u):
        return jnp.clip(row0(u) - NA_WIN_ROWS // 2, 0, rows - KR)

    def key_slice(u):
        return pl.ds(pl.multiple_of(key_row0(u) * W, W), KR * W)

    def scores(u):
        kblk = k_ref[key_slice(u), :]
        parts = []
        for qr in range(R):
            qrow = q_ref[W * (R * u + qr):W * (R * u + qr + 1), :]
            parts.append(jnp.where(head0, qrow, zero))
            parts.append(jnp.where(head0, zero, qrow))
        qcat = jnp.concatenate(parts, axis=0)
        return lax.dot_general(kblk, qcat, (((1,), (1,)), ((), ())), preferred_element_type=F32)

    def softmax(u, st):
        r0, ks = row0(u), key_row0(u)
        blocks = []
        for a in range(KR):
            rowb = []
            for qr in range(R):
                krow = ks + a
                r = r0 + qr
                rs = jnp.clip(r - NA_WIN_ROWS // 2, 0, rows - NA_WIN_ROWS)
                ok = jnp.logical_and(krow >= rs, krow < rs + NA_WIN_ROWS)
                idx = jnp.where(ok, krow - r + (NA_WIN_ROWS - 1), 2 * NA_WIN_ROWS - 1)
                rowb.append(st[W * a:W * (a + 1), LANES * qr:LANES * (qr + 1)] + tbl_ref[idx])
            blocks.append(jnp.concatenate(rowb, axis=1))
        s = jnp.concatenate(blocks, axis=0)
        m = jnp.max(s, axis=0, keepdims=True)
        return jnp.exp(s - m).astype(BF16)

    def values(u, p):
        vblk = v_ref[key_slice(u), :]
        vaug = jnp.concatenate([vblk, jnp.ones_like(vblk)], axis=1)
        o = lax.dot_general(p, vaug, (((0,), (0,)), ((), ())), preferred_element_type=F32)
        o = o[:, :LANES] / o[:, LANES:]
        for qr in range(R):
            top = o[2 * W * qr:2 * W * qr + W, :]
            bot = o[2 * W * qr + W:2 * W * (qr + 1), :]
            o_ref[W * (R * u + qr):W * (R * u + qr + 1), :] = jnp.where(head0, top, bot).astype(BF16)

    st = {u: scores(u) for u in range(min(2, U))}
    pending = None
    for u in range(U):
        p = softmax(u, st.pop(u))
        if u + 2 < U:
            st[u + 2] = scores(u + 2)
        if pending is not None:
            values(u - 1, pending)
        pending = p
    values(U - 1, pending)


def _na_call(qn, kn, big, tbl, batch, seq_len):
    t = qn.shape[0]
    rows = seq_len // GRID_W
    R, U = NA_ROWS_PER_STEP, NA_UNITS_PER_STEP
    nst = rows // (R * U)
    kern = functools.partial(_na_kernel, rows=rows, R=R, U=U)
    return pl.pallas_call(
        kern,
        out_shape=jax.ShapeDtypeStruct((t, BRANCH_W), BF16),
        grid=(batch, NA_HEADS // 2, nst),
        in_specs=[
            pl.BlockSpec((U * R * GRID_W, LANES), lambda b, hp, i: (b * nst + i, hp)),
            pl.BlockSpec((seq_len, LANES), lambda b, hp, i: (b, hp)),
            pl.BlockSpec((seq_len, LANES), lambda b, hp, i: (b, C_NV // LANES + hp)),
            pl.BlockSpec((None, 2 * NA_WIN_ROWS, GRID_W, LANES), lambda b, hp, i: (hp, 0, 0, 0)),
        ],
        out_specs=pl.BlockSpec((U * R * GRID_W, LANES), lambda b, hp, i: (b * nst + i, hp)),
        compiler_params=_cparams(("arbitrary", "arbitrary", "arbitrary")),
        name="na_attn",
    )(qn, kn, big, tbl)


def _mlstm_gates(sm_ref, smt_ref, gb_r, gb_c, backward):
    L = CHUNK
    g_c = sm_ref[...] + gb_r[...]
    g_r = smt_ref[...] + gb_c[...]
    m_c = _tri(L, lower=not backward).astype(BF16)
    m_r = _tri(L, lower=backward).astype(BF16)
    bcs_c = _dot_exact_lhs(m_c, -_softplus(-g_c))
    bcs_r = _dot_exact_rhs(-_softplus(-g_r), m_r)
    return g_c, g_r, bcs_c, bcs_r, _tri(L, lower=not backward)


def _mlstm_unit(refs, gates, d, h, backward):
    q_ref, k_ref, v_ref, c_scr, n_scr, m_scr, o_ref = refs
    g_c, g_r, bcs_c_all, bcs_r_all, valid = gates
    L = CHUNK
    ii = 2 * ML_HEADS * d + h
    fi = ii + ML_HEADS
    sl = slice(ML_HEAD_DIM * h, ML_HEAD_DIM * (h + 1))
    t = {}

    def stage_a():
        t["qh"] = q_ref[:, sl]
        t["kh"] = k_ref[:, sl] * (ML_HEAD_DIM ** -0.5)
        t["sc"] = lax.dot_general(t["qh"], t["kh"], (((1,), (1,)), ((), ())), preferred_element_type=F32)
        t["c_prev"] = c_scr[h]
        t["qc"] = jnp.dot(t["qh"], t["c_prev"].astype(BF16), preferred_element_type=F32)

    def stage_b():
        li_c = g_c[:, S_MG + ii:S_MG + ii + 1]
        li_r = g_r[ii:ii + 1, :]
        bcs_c = bcs_c_all[:, S_MG + fi:S_MG + fi + 1]
        bcs_r = bcs_r_all[fi:fi + 1, :]
        b_end = bcs_c[0:1, :] if backward else bcs_c[L - 1:L, :]
        dm = jnp.where(valid, bcs_c - (bcs_r - li_r), -jnp.inf)
        m_prev = m_scr[h, 0:1, 0:1]
        inter = bcs_c + m_prev
        m_t = jnp.maximum(inter, jnp.max(dm, axis=1, keepdims=True))
        w_intra = jnp.exp(dm - m_t) * t.pop("sc")
        t["w_inter"] = jnp.exp(inter - m_t)
        t["n_prev"] = n_scr[h, 0:1, :]
        den = (jnp.sum(w_intra, axis=1, keepdims=True)
               + t["w_inter"] * jnp.sum(t["qh"].astype(F32) * t["n_prev"], axis=1, keepdims=True))
        t["den"] = jnp.maximum(jnp.abs(den), jnp.exp(-m_t))
        t["w_intra"] = w_intra.astype(BF16)
        g_end_r = b_end - bcs_r + li_r
        m_loc = jnp.max(g_end_r, axis=1, keepdims=True)
        w_end_c = jnp.exp(b_end - bcs_c + li_c - m_loc)
        kw = t.pop("kh").astype(F32) * w_end_c
        t["n_loc"] = jnp.sum(kw, axis=0, keepdims=True)
        t["kw"] = kw.astype(BF16)
        m_new = jnp.maximum(b_end + m_prev, m_loc)
        t["a_old"] = jnp.exp(b_end + m_prev - m_new)
        t["a_loc"] = jnp.exp(m_loc - m_new)
        t["m_new"] = m_new

    def stage_c():
        vh = v_ref[:, sl]
        t["wv"] = jnp.dot(t.pop("w_intra"), vh, preferred_element_type=F32)
        t["c_loc"] = lax.dot_general(t.pop("kw"), vh, (((0,), (0,)), ((), ())), preferred_element_type=F32)

    def stage_d():
        num = t["wv"] + t["w_inter"] * t["qc"]
        o_ref[:, sl] = (num / t["den"]).astype(BF16)
        c_scr[h] = t["a_old"] * t["c_prev"] + t["a_loc"] * t["c_loc"]
        n_scr[h] = jnp.broadcast_to(t["a_old"] * t["n_prev"] + t["a_loc"] * t["n_loc"], n_scr.shape[1:])
        m_scr[h] = jnp.broadcast_to(t["m_new"], m_scr.shape[1:])

    return stage_a, stage_b, stage_c, stage_d


def _mlstm_kernel(qf, kf, vf, smf, smtf, qb, kb, vb, smb, smtb, gb_r, gb_c, hf_ref, hb_ref,
                  cf, nf, mf, cb, nb, mb):
    @pl.when(pl.program_id(1) == 0)
    def _():
        for r in (cf, nf, mf, cb, nb, mb):
            r[...] = jnp.zeros_like(r)

    gates_f = _mlstm_gates(smf, smtf, gb_r, gb_c, False)
    gates_b = _mlstm_gates(smb, smtb, gb_r, gb_c, True)
    units = []
    for h in range(ML_HEADS):
        units.append(_mlstm_unit((qf, kf, vf, cf, nf, mf, hf_ref), gates_f, 0, h, False))
        units.append(_mlstm_unit((qb, kb, vb, cb, nb, mb, hb_ref), gates_b, 1, h, True))
    n = len(units)
    for u in range(min(2, n)):
        units[u][0]()
    for u in range(n):
        units[u][1]()
        if u + 2 < n:
            units[u + 2][0]()
        units[u][2]()
        if u >= 1:
            units[u - 1][3]()
    units[n - 1][3]()


def _mlstm_call(big, small, small_t, gb_r, gb_c, batch, seq_len):
    t = big.shape[0]
    L = CHUNK
    nc = seq_len // L
    cq, ck, cv = C_MQ // BRANCH_W, C_MK // BRANCH_W, C_MV // BRANCH_W
    f = lambda c: (lambda b, j: (b * nc + j, c))
    r = lambda c: (lambda b, j: (b * nc + (nc - 1 - j), c))
    const = lambda b, j: (0, 0)
    state = [pltpu.VMEM((ML_HEADS, ML_HEAD_DIM, ML_HEAD_DIM), F32), pltpu.VMEM((ML_HEADS, 8, ML_HEAD_DIM), F32),
             pltpu.VMEM((ML_HEADS, 8, LANES), F32)]
    return pl.pallas_call(
        _mlstm_kernel,
        out_shape=(jax.ShapeDtypeStruct((t, BRANCH_W), BF16), jax.ShapeDtypeStruct((t, BRANCH_W), BF16)),
        grid=(batch, nc),
        in_specs=[
            pl.BlockSpec((L, BRANCH_W), f(cq)), pl.BlockSpec((L, BRANCH_W), f(ck)), pl.BlockSpec((L, BRANCH_W), f(cv)),
            pl.BlockSpec((L, LANES), f(0)), pl.BlockSpec((16, L), lambda b, j: (S_MG // 16, b * nc + j)),
            pl.BlockSpec((L, BRANCH_W), r(cq)), pl.BlockSpec((L, BRANCH_W), r(ck)), pl.BlockSpec((L, BRANCH_W), r(cv)),
            pl.BlockSpec((L, LANES), r(0)), pl.BlockSpec((16, L), lambda b, j: (S_MG // 16, b * nc + (nc - 1 - j))),
            pl.BlockSpec((1, LANES), const), pl.BlockSpec((16, 1), const),
        ],
        out_specs=(pl.BlockSpec((L, BRANCH_W), f(0)), pl.BlockSpec((L, BRANCH_W), r(0))),
        scratch_shapes=state + state,
        compiler_params=_cparams(("arbitrary", "arbitrary")),
        name="mlstm_scan",
    )(big, big, big, small, small_t, big, big, big, small, small_t, gb_r, gb_c)


def _merge_kernel(yf_ref, yb_ref, z_ref, og_ref, on_ref, hf_ref, hb_ref, mo_ref, gate_ref, snw_ref, mnw_ref,
                  wb_ref, o_ref):
    y = (yf_ref[...].astype(F32) + yb_ref[...].astype(F32)) * _silu(z_ref[...].astype(F32))
    ms = jnp.mean(y * y, axis=-1, keepdims=True)
    y0 = (y * lax.rsqrt(ms + EPS) * snw_ref[...]).astype(BF16)
    parts = []
    for h in range(ML_HEADS):
        sl = slice(ML_HEAD_DIM * h, ML_HEAD_DIM * (h + 1))
        hh = hf_ref[:, sl].astype(F32) + hb_ref[:, sl].astype(F32)
        ms = jnp.mean(hh * hh, axis=-1, keepdims=True)
        parts.append(hh * lax.rsqrt(ms + EPS) * mnw_ref[:, sl])
    y3 = (jax.nn.sigmoid(mo_ref[...].astype(F32)) * jnp.concatenate(parts, axis=1)).astype(BF16)
    acc = None
    for i, yi in enumerate((y0, og_ref[...], on_ref[...], y3)):
        g = gate_ref[:, D_MODEL * i:D_MODEL * (i + 1)].astype(F32)
        term = g * jnp.dot(yi, wb_ref[i], preferred_element_type=F32)
        acc = term if acc is None else acc + term
    o_ref[...] = acc.astype(BF16)


def _merge_call(yf, yb, og, on, hf, hb, big, snw, mnw, wb):
    t = big.shape[0]
    tm = 256
    row = lambda i: (i, 0)
    const2 = lambda i: (0, 0)
    return pl.pallas_call(
        _merge_kernel,
        out_shape=jax.ShapeDtypeStruct((t, D_MODEL), BF16),
        grid=(t // tm,),
        in_specs=[
            pl.BlockSpec((tm, BRANCH_W), row), pl.BlockSpec((tm, BRANCH_W), row),
            pl.BlockSpec((tm, BRANCH_W), lambda i: (i, C_Z // BRANCH_W)),
            pl.BlockSpec((tm, BRANCH_W), row), pl.BlockSpec((tm, BRANCH_W), row),
            pl.BlockSpec((tm, BRANCH_W), row), pl.BlockSpec((tm, BRANCH_W), row),
            pl.BlockSpec((tm, BRANCH_W), lambda i: (i, C_MO // BRANCH_W)),
            pl.BlockSpec((tm, N_BRANCH * D_MODEL), lambda i: (i, 0)),
            pl.BlockSpec((1, BRANCH_W), const2), pl.BlockSpec((1, BRANCH_W), const2),
            pl.BlockSpec((N_BRANCH, BRANCH_W, D_MODEL), lambda i: (0, 0, 0), pipeline_mode=pl.Buffered(1)),
        ],
        out_specs=pl.BlockSpec((tm, D_MODEL), row),
        compiler_params=_cparams(("arbitrary",)),
        name="merge",
    )(yf, yb, big, og, on, hf, hb, big, big, snw, mnw, wb)


def _resproj_kernel(a_ref, w_ref, x_ref, mod_ref, o_ref, *, mod_row):
    acc = jnp.dot(a_ref[...], w_ref[...], preferred_element_type=F32)
    o_ref[...] = x_ref[...] + mod_ref[mod_row:mod_row + 1, :] * acc


def _resproj_call(a, w, x, mod, mod_row, seq_len, tm, tn):
    t, k = a.shape
    n = w.shape[1]
    tiles_per_seq = seq_len // tm
    kern = functools.partial(_resproj_kernel, mod_row=mod_row)
    return pl.pallas_call(
        kern,
        out_shape=jax.ShapeDtypeStruct((t, n), F32),
        grid=(n // tn, t // tm),
        in_specs=[
            pl.BlockSpec((tm, k), lambda j, i: (i, 0)),
            pl.BlockSpec((k, tn), lambda j, i: (0, j)),
            pl.BlockSpec((tm, tn), lambda j, i: (i, j)),
            pl.BlockSpec((None, 6, tn), lambda j, i: (i // tiles_per_seq, 0, j)),
        ],
        out_specs=pl.BlockSpec((tm, tn), lambda j, i: (i, j)),
        compiler_params=_cparams(("arbitrary", "arbitrary")),
        name="resproj",
    )(a, w, x, mod)


def _ffn_in_kernel(x_ref, mod_ref, nw_ref, wu_ref, wg_ref, o_ref, h_scr, *, tm, rc):
    @pl.when(pl.program_id(1) == 0)
    def _():
        nw = nw_ref[...]
        sh = mod_ref[3:4, :]
        sc = 1.0 + mod_ref[4:5, :]

        def body(r, carry):
            r0 = pl.multiple_of(r * rc, rc)
            h_scr[pl.ds(r0, rc), :] = _norm_mod_rows(x_ref, nw, sc, sh, r0, rc).astype(BF16)
            return carry

        lax.fori_loop(0, tm // rc, body, 0)

    h = h_scr[...]
    up = jnp.dot(h, wu_ref[...], preferred_element_type=F32)
    gate = jnp.dot(h, wg_ref[...], preferred_element_type=F32)
    o_ref[...] = (_silu(gate) * up).astype(BF16)


def _ffn_in_call(x, mod, nw, w_ffn_in, seq_len):
    t = x.shape[0]
    tm, tn, rc = 1024, 512, 128
    nt = FFN_HIDDEN // tn
    tiles_per_seq = seq_len // tm
    kern = functools.partial(_ffn_in_kernel, tm=tm, rc=rc)
    return pl.pallas_call(
        kern,
        out_shape=jax.ShapeDtypeStruct((t, FFN_HIDDEN), BF16),
        grid=(t // tm, nt),
        in_specs=[
            pl.BlockSpec((tm, D_MODEL), lambda i, j: (i, 0)),
            pl.BlockSpec((None, 6, D_MODEL), lambda i, j: (i // tiles_per_seq, 0, 0)),
            pl.BlockSpec((1, D_MODEL), lambda i, j: (0, 0)),
            pl.BlockSpec((D_MODEL, tn), lambda i, j: (0, j)),
            pl.BlockSpec((D_MODEL, tn), lambda i, j: (0, nt + j)),
        ],
        out_specs=pl.BlockSpec((tm, tn), lambda i, j: (i, j)),
        scratch_shapes=[pltpu.VMEM((tm, D_MODEL), BF16)],
        compiler_params=_cparams(("arbitrary", "arbitrary")),
        name="ffn_in",
    )(x, mod, nw, w_ffn_in, w_ffn_in)


def _rope_tables(seq_len):
    pos = np.arange(seq_len)
    rows, cols = pos // GRID_W, pos % GRID_W
    d = GQA_HEAD_DIM // 2
    inv = np.asarray(ROPE_THETA ** (-np.arange(0, d, 2, dtype=np.float32) / d), dtype=np.float32)

    def cs(p):
        ang = p[:, None].astype(np.float32) * inv[None]
        return np.cos(ang), np.sin(ang)

    cr, sr = cs(rows)
    cc, sc = cs(cols)
    cos = np.concatenate([cr, cr, cc, cc], axis=1).astype(np.float32)
    sin = np.concatenate([-sr, sr, -sc, sc], axis=1).astype(np.float32)
    return jnp.asarray(cos), jnp.asarray(sin)


def _na_bias_table(rpb):
    c = np.arange(GRID_W)[:, None]
    j = np.arange(GRID_W)[None, :]
    start = np.clip(j - NA_WIN_COLS // 2, 0, GRID_W - NA_WIN_COLS)
    ok = (c >= start) & (c < start + NA_WIN_COLS)
    dc = np.clip(c - j + (NA_WIN_COLS - 1), 0, 2 * NA_WIN_COLS - 2)
    t = rpb[:, :, dc]
    t = jnp.where(jnp.asarray(ok)[None, None], t, NEG)
    t = t.reshape(NA_HEADS // 2, 2, 2 * NA_WIN_ROWS - 1, GRID_W, GRID_W)
    t = t.transpose(0, 2, 3, 1, 4).reshape(NA_HEADS // 2, 2 * NA_WIN_ROWS - 1, GRID_W, LANES)
    pad = jnp.full((NA_HEADS // 2, 1, GRID_W, LANES), NEG, F32)
    return jnp.concatenate([t, pad], axis=1)


def _lane_row(v, offset):
    return jnp.zeros((1, LANES), F32).at[0, offset:offset + v.shape[0]].set(v)


def _layer_weights(l, w_in, w_bgate, b_bgate):
    wi = w_in[l]
    o = np.cumsum((0, 1024, 1536, 32, 1024, 256, 256, 1024, 1024, 1024, 1024, 1024, 1024, 1024, 16))
    z, xbc, dt, gq, gk, gv, nq, nk, nv, mq, mk, mv, mo, mg = [wi[:, o[i]:o[i + 1]] for i in range(14)]
    w_big = jnp.concatenate([w_bgate[l], z, xbc[:, :1024], gq, nq, nk, nv, mq, mk, mv, mo, xbc[:, 1024:], gk, gv],
                            axis=1).astype(BF16)
    b_big = jnp.concatenate([b_bgate[l], jnp.zeros((BIG_COLS - N_BRANCH * D_MODEL,), F32)])[None, :]
    ws = jnp.concatenate([dt, mg, jnp.zeros((D_MODEL, LANES - 48), F32)], axis=1)
    ws_hi = ws.astype(BF16)
    ws_lo = (ws - ws_hi.astype(F32)).astype(BF16)
    return w_big, b_big, ws_hi, ws_lo


def _trunk_group(x, mods, seq_len, batch, consts, layers):
    for l in range(DEPTH):
        lw = layers[l]
        mod = mods[l]
        big, small = _inproj_call(x, mod, lw["norm1"], lw["w_big"], lw["b_big"], lw["ws_hi"], lw["ws_lo"], seq_len)
        small_t = small.T
        xs_c, bc_c = _conv_call(big, lw["conv_w1"], lw["conv_b1"], lw["conv_w2"], lw["conv_b2"], seq_len)
        yf, yb = _ssd_call(xs_c, bc_c, small, small_t, lw["dtb_r"], lw["dtb_c"], lw["alog_r"], lw["alog_c"],
                           lw["dskip"], consts["expand"], batch, seq_len)
        qp, kp, vt = _gqa_prep_call(big, consts["cos"][seq_len], consts["sin"][seq_len], lw["gqa_qw"], lw["gqa_kw"],
                                    seq_len)
        og = _flash_call(qp, kp, vt, batch, seq_len)
        qn, kn = _na_prep_call(big, lw["na_qw"], lw["na_kw"])
        on = _na_call(qn, kn, big, lw["na_tbl"], batch, seq_len)
        hf, hb = _mlstm_call(big, small, small_t, lw["ml_gb_r"], lw["ml_gb_c"], batch, seq_len)
        merged = _merge_call(yf, yb, og, on, hf, hb, big, lw["ssd_nw"], lw["ml_nw"], lw["wb"])
        x = _resproj_call(merged, lw["w_o"], x, mod, 2, seq_len, 512, 1024)
        act = _ffn_in_call(x, mod, lw["norm2"], lw["w_ffn_in"], seq_len)
        x = _resproj_call(act, lw["w_ffn_out"], x, mod, 5, seq_len, 512, 1024)
    return x


def kernel(x_prompt, x_sample, c_prompt, c_sample, w_ada, b_ada, norm1_w, w_in, w_bgate, b_bgate, ssd_conv_w,
           ssd_conv_b, ssd_dt_bias, ssd_a_log, ssd_d, ssd_norm_w, gqa_q_norm, gqa_k_norm, na_q_norm, na_k_norm,
           na_rpb, ml_gate_bias, ml_out_norm, w_branch_out, w_o, norm2_w, w_ffn_in, w_ffn_out):
    bp, sp, _ = x_prompt.shape
    bs, ss, _ = x_sample.shape
    nseq = bp + bs
    rows = -(-nseq // 16) * 16
    c_all = jnp.concatenate([c_prompt, c_sample, jnp.zeros((rows - nseq, D_MODEL), F32)], axis=0)
    mods = _ada_call(c_all, w_ada.astype(BF16), b_ada.reshape(DEPTH, 1, 6 * D_MODEL))
    mods = mods.reshape(DEPTH, rows, 6, D_MODEL)

    expand = np.zeros((2, LANES, BRANCH_W), np.float32)
    for d in range(2):
        for h in range(SSD_HEADS):
            expand[d, SSD_HEADS * d + h, SSD_HEAD_DIM * h:SSD_HEAD_DIM * (h + 1)] = 1.0
    consts = {"expand": jnp.asarray(expand, BF16), "cos": {}, "sin": {}}
    for s in {sp, ss}:
        consts["cos"][s], consts["sin"][s] = _rope_tables(s)

    layers = []
    for l in range(DEPTH):
        w_big, b_big, ws_hi, ws_lo = _layer_weights(l, w_in, w_bgate, b_bgate)
        layers.append({
            "norm1": norm1_w[l][None, :], "norm2": norm2_w[l][None, :],
            "w_big": w_big, "b_big": b_big, "ws_hi": ws_hi, "ws_lo": ws_lo,
            "conv_w1": ssd_conv_w[l][:, :BRANCH_W], "conv_b1": ssd_conv_b[l][None, :BRANCH_W],
            "conv_w2": ssd_conv_w[l][:, BRANCH_W:], "conv_b2": ssd_conv_b[l][None, BRANCH_W:],
            "dtb_r": _lane_row(ssd_dt_bias[l].reshape(-1), S_DT), "dtb_c": ssd_dt_bias[l].reshape(2 * SSD_HEADS, 1),
            "alog_r": _lane_row(ssd_a_log[l].reshape(-1), S_DT), "alog_c": ssd_a_log[l].reshape(2 * SSD_HEADS, 1),
            "dskip": jnp.repeat(ssd_d[l], SSD_HEAD_DIM)[None, :],
            "ssd_nw": ssd_norm_w[l][None, :],
            "gqa_qw": gqa_q_norm[l][None, :], "gqa_kw": gqa_k_norm[l][None, :],
            "na_qw": jnp.tile(na_q_norm[l], 2)[None, :], "na_kw": jnp.tile(na_k_norm[l], 2)[None, :],
            "na_tbl": _na_bias_table(na_rpb[l]),
            "ml_gb_r": _lane_row(ml_gate_bias[l].reshape(-1), S_MG), "ml_gb_c": ml_gate_bias[l].reshape(16, 1),
            "ml_nw": ml_out_norm[l][None, :],
            "wb": w_branch_out[l].astype(BF16), "w_o": w_o[l].astype(BF16),
            "w_ffn_in": w_ffn_in[l].astype(BF16), "w_ffn_out": w_ffn_out[l].astype(BF16),
        })

    y_p = _trunk_group(x_prompt.reshape(bp * sp, D_MODEL), mods[:, :bp], sp, bp, consts, layers)
    y_s = _trunk_group(x_sample.reshape(bs * ss, D_MODEL), mods[:, bp:bp + bs], ss, bs, consts, layers)
    return y_p.reshape(bp, sp, D_MODEL), y_s.reshape(bs, ss, D_MODEL)
```

```python
import functools
import math

import numpy as np
import jax
import jax.numpy as jnp
from jax import lax
from jax.experimental import pallas as pl
from jax.experimental.pallas import tpu as pltpu

F32 = jnp.float32
BF16 = jnp.bfloat16

D_MODEL = 2048
DEPTH = 4
GRID_W = 64
EPS = 1e-6
N_BRANCH = 4
BRANCH_W = 1024
SSD_HEAD_DIM = 64
SSD_HEADS = 16
SSD_GROUPS = 2
SSD_STATE = 128
SSD_CONV = 5
CHUNK = 128
GQA_HEAD_DIM = 128
GQA_HEADS = 8
GQA_KV_HEADS = 2
ROPE_THETA = 10000.0
NA_HEAD_DIM = 64
NA_HEADS = 16
NA_WIN_ROWS = 8
NA_WIN_COLS = 16
ML_HEADS = 4
ML_HEAD_DIM = 256
FFN_HIDDEN = 5632

VMEM_LIMIT_BYTES = 56 * 1024 * 1024
LANES = 128

C_GATE = 0
C_Z = 8192
C_XS = 9216
C_GQ = 10240
C_NQ = 11264
C_NK = 12288
C_NV = 13312
C_MQ = 14336
C_MK = 15360
C_MV = 16384
C_MO = 17408
C_BM = 18432
C_GK = 18944
C_GV = 19200
BIG_COLS = 19456
S_DT = 0
S_MG = 32
NEG = -1e30


def _cparams(sem):
    return pltpu.CompilerParams(dimension_semantics=sem, vmem_limit_bytes=VMEM_LIMIT_BYTES)


def _silu(x):
    return x * jax.nn.sigmoid(x)


def _softplus(x):
    return jnp.maximum(x, 0.0) + jnp.log1p(jnp.exp(-jnp.abs(x)))


def _split3(x):
    x1 = x.astype(BF16)
    r1 = x - x1.astype(F32)
    x2 = r1.astype(BF16)
    x3 = (r1 - x2.astype(F32)).astype(BF16)
    return x1, x2, x3


def _dot_exact_rhs(x, m):
    a, b, c = _split3(x)
    d = lambda t: jnp.dot(t, m, preferred_element_type=F32)
    return d(a) + d(b) + d(c)


def _dot_exact_lhs(m, x):
    a, b, c = _split3(x)
    d = lambda t: jnp.dot(m, t, preferred_element_type=F32)
    return d(a) + d(b) + d(c)


def _tri(n, lower):
    r = lax.broadcasted_iota(jnp.int32, (n, n), 0)
    c = lax.broadcasted_iota(jnp.int32, (n, n), 1)
    return (c <= r) if lower else (c >= r)


def _ada_kernel(c_ref, w_ref, b_ref, o_ref):
    a = _silu(c_ref[...]).astype(BF16)
    o_ref[...] = jnp.dot(a, w_ref[...], preferred_element_type=F32) + b_ref[...]


def _ada_call(c_pad, w_ada, b_ada):
    rows = c_pad.shape[0]
    tn = 2048
    nt = w_ada.shape[-1] // tn
    return pl.pallas_call(
        _ada_kernel,
        out_shape=jax.ShapeDtypeStruct((DEPTH, rows, w_ada.shape[-1]), F32),
        grid=(DEPTH, nt),
        in_specs=[
            pl.BlockSpec((rows, D_MODEL), lambda l, j: (0, 0)),
            pl.BlockSpec((None, D_MODEL, tn), lambda l, j: (l, 0, j)),
            pl.BlockSpec((None, 1, tn), lambda l, j: (l, 0, j)),
        ],
        out_specs=pl.BlockSpec((None, rows, tn), lambda l, j: (l, 0, j)),
        compiler_params=_cparams(("arbitrary", "arbitrary")),
        name="ada_mod",
    )(c_pad, w_ada, b_ada)


def _norm_mod_rows(x_ref, nw, sc, sh, r0, rc):
    xf = x_ref[pl.ds(r0, rc), :]
    ms = jnp.mean(xf * xf, axis=-1, keepdims=True)
    return xf * lax.rsqrt(ms + EPS) * nw * sc + sh


def _inproj_kernel(x_ref, mod_ref, nw_ref, w_ref, b_ref, wsh_ref, wsl_ref, big_ref, small_ref, h_scr,
                   *, tm, rc, n_gate_tiles):
    j = pl.program_id(1)

    @pl.when(j == 0)
    def _():
        nw = nw_ref[...]
        sh = mod_ref[0:1, :]
        sc = 1.0 + mod_ref[1:2, :]

        def body(r, carry):
            r0 = pl.multiple_of(r * rc, rc)
            hf = _norm_mod_rows(x_ref, nw, sc, sh, r0, rc)
            hb = hf.astype(BF16)
            h_scr[pl.ds(r0, rc), :] = hb
            lo = (hf - hb.astype(F32)).astype(BF16)
            wh = wsh_ref[...]
            sm = (jnp.dot(hb, wh, preferred_element_type=F32)
                  + jnp.dot(lo, wh, preferred_element_type=F32)
                  + jnp.dot(hb, wsl_ref[...], preferred_element_type=F32))
            small_ref[pl.ds(r0, rc), :] = sm
            return carry

        lax.fori_loop(0, tm // rc, body, 0)

    acc = jnp.dot(h_scr[...], w_ref[...], preferred_element_type=F32) + b_ref[...]
    big_ref[...] = jnp.where(j < n_gate_tiles, jax.nn.sigmoid(acc), acc).astype(BF16)


def _inproj_call(x, mod, nw, w_big, b_big, ws_hi, ws_lo, seq_len):
    t = x.shape[0]
    tm, rc = 1024, 128
    tn = w_big.shape[-1]
    tiles_per_seq = seq_len // tm
    kern = functools.partial(_inproj_kernel, tm=tm, rc=rc, n_gate_tiles=(N_BRANCH * D_MODEL) // tn)
    return pl.pallas_call(
        kern,
        out_shape=(jax.ShapeDtypeStruct((t, BIG_COLS), BF16), jax.ShapeDtypeStruct((t, LANES), F32)),
        grid=(t // tm, BIG_COLS // tn),
        in_specs=[
            pl.BlockSpec((tm, D_MODEL), lambda i, j: (i, 0)),
            pl.BlockSpec((None, 6, D_MODEL), lambda i, j: (i // tiles_per_seq, 0, 0)),
            pl.BlockSpec((1, D_MODEL), lambda i, j: (0, 0)),
            pl.BlockSpec((None, D_MODEL, tn), lambda i, j: (j, 0, 0)),
            pl.BlockSpec((1, tn), lambda i, j: (0, j)),
            pl.BlockSpec((D_MODEL, LANES), lambda i, j: (0, 0)),
            pl.BlockSpec((D_MODEL, LANES), lambda i, j: (0, 0)),
        ],
        out_specs=(pl.BlockSpec((tm, tn), lambda i, j: (i, j)),
                   pl.BlockSpec((tm, LANES), lambda i, j: (i, 0))),
        scratch_shapes=[pltpu.VMEM((tm, D_MODEL), BF16)],
        compiler_params=_cparams(("arbitrary", "arbitrary")),
        name="inproj",
    )(x, mod, nw, w_big, b_big, ws_hi, ws_lo)


HALO = 16


def _conv_kernel(xs_ref, xsp_ref, xsn_ref, bc_ref, bcp_ref, bcn_ref, w1_ref, b1_ref, w2_ref, b2_ref,
                 oxs_ref, obc_ref, ext1, ext2, *, tm, rc, tiles_per_seq):
    i = pl.program_id(0)
    first = (i % tiles_per_seq) == 0
    last = (i % tiles_per_seq) == (tiles_per_seq - 1)
    pad = SSD_CONV // 2
    for main, prv, nxt, w_ref, b_ref, out, ext in (
            (xs_ref, xsp_ref, xsn_ref, w1_ref, b1_ref, oxs_ref, ext1),
            (bc_ref, bcp_ref, bcn_ref, w2_ref, b2_ref, obc_ref, ext2)):
        ext[HALO:HALO + tm, :] = main[...].astype(F32)
        ext[0:HALO, :] = jnp.where(first, 0.0, prv[...].astype(F32))
        ext[HALO + tm:2 * HALO + tm, :] = jnp.where(last, 0.0, nxt[...].astype(F32))
        for c in range(tm // rc):
            acc = jnp.zeros((rc, ext.shape[1]), F32) + b_ref[...]
            for k in range(SSD_CONV):
                s = HALO + c * rc + k - pad
                acc = acc + w_ref[k:k + 1, :] * ext[s:s + rc, :]
            out[c * rc:(c + 1) * rc, :] = _silu(acc).astype(BF16)


def _conv_call(big, w1, b1, w2, b2, seq_len):
    t = big.shape[0]
    tm, rc = 512, 64
    tiles_per_seq = seq_len // tm
    nb = t // HALO
    hb = tm // HALO
    kern = functools.partial(_conv_kernel, tm=tm, rc=rc, tiles_per_seq=tiles_per_seq)
    prev_map = lambda cb: (lambda i: (jnp.maximum(i * hb - 1, 0), cb))
    next_map = lambda cb: (lambda i: (jnp.minimum((i + 1) * hb, nb - 1), cb))
    cxs = C_XS // BRANCH_W
    cbc = C_BM // 512
    return pl.pallas_call(
        kern,
        out_shape=(jax.ShapeDtypeStruct((t, BRANCH_W), BF16), jax.ShapeDtypeStruct((t, 512), BF16)),
        grid=(t // tm,),
        in_specs=[
            pl.BlockSpec((tm, BRANCH_W), lambda i: (i, cxs)),
            pl.BlockSpec((HALO, BRANCH_W), prev_map(cxs)),
            pl.BlockSpec((HALO, BRANCH_W), next_map(cxs)),
            pl.BlockSpec((tm, 512), lambda i: (i, cbc)),
            pl.BlockSpec((HALO, 512), prev_map(cbc)),
            pl.BlockSpec((HALO, 512), next_map(cbc)),
            pl.BlockSpec((SSD_CONV, BRANCH_W), lambda i: (0, 0)),
            pl.BlockSpec((1, BRANCH_W), lambda i: (0, 0)),
            pl.BlockSpec((SSD_CONV, 512), lambda i: (0, 0)),
            pl.BlockSpec((1, 512), lambda i: (0, 0)),
        ],
        out_specs=(pl.BlockSpec((tm, BRANCH_W), lambda i: (i, 0)),
                   pl.BlockSpec((tm, 512), lambda i: (i, 0))),
        scratch_shapes=[pltpu.VMEM((tm + 2 * HALO, BRANCH_W), F32), pltpu.VMEM((tm + 2 * HALO, 512), F32)],
        compiler_params=_cparams(("arbitrary",)),
        name="ssd_conv",
    )(big, big, big, big, big, big, w1, b1, w2, b2)


def _ssd_direction(xs_ref, bc_ref, sm_ref, smt_ref, dtb_r, dtb_c, alog_r, alog_c, exp_ref, st_ref, d, backward):
    L = CHUNK
    hs = slice(SSD_HEADS * d, SSD_HEADS * (d + 1))
    dt_c = _softplus(sm_ref[...] + dtb_r[...])
    da_c = dt_c * (-jnp.exp(alog_r[...]))
    dt_r = _softplus(smt_ref[hs, :] + dtb_c[hs, :])
    da_r = dt_r * (-jnp.exp(alog_c[hs, :]))
    m_c = _tri(L, lower=not backward).astype(BF16)
    m_r = _tri(L, lower=backward).astype(BF16)
    acs_c = _dot_exact_lhs(m_c, da_c)
    acs_r = _dot_exact_rhs(da_r, m_r)
    valid = _tri(L, lower=not backward)
    acs_last = acs_c[0:1, :] if backward else acs_c[L - 1:L, :]
    w_c = dt_c * jnp.exp(acs_last - acs_c)
    e_c = jnp.exp(acs_c)
    expand = exp_ref[d]

    def widen(v):
        hi = v.astype(BF16)
        lo = (v - hi.astype(F32)).astype(BF16)
        return (jnp.dot(hi, expand, preferred_element_type=F32)
                + jnp.dot(lo, expand, preferred_element_type=F32))

    w_wide = widen(w_c)
    e_wide = widen(e_c)
    cd_wide = e_wide[0:1, :] if backward else e_wide[L - 1:L, :]

    xs = xs_ref[...]
    xs_f = xs.astype(F32)
    lane = lax.broadcasted_iota(jnp.int32, (L, LANES), 1)
    gw = BRANCH_W // SSD_GROUPS
    hpg = SSD_HEADS // SSD_GROUPS
    ys = []
    for g in range(SSD_GROUPS):
        bm = bc_ref[:, SSD_STATE * g:SSD_STATE * (g + 1)]
        cm = bc_ref[:, SSD_GROUPS * SSD_STATE + SSD_STATE * g:SSD_GROUPS * SSD_STATE + SSD_STATE * (g + 1)]
        cb = lax.dot_general(cm, bm, (((1,), (1,)), ((), ())), preferred_element_type=F32)
        state = st_ref[g]
        y_off = jnp.dot(cm, state.astype(BF16), preferred_element_type=F32)
        cols = []
        for hp in range(hpg // 2):
            x_pair = xs[:, gw * g + LANES * hp:gw * g + LANES * (hp + 1)]
            acc = None
            for sub in range(2):
                h = hpg * g + 2 * hp + sub
                hc = SSD_HEADS * d + h
                dm = acs_c[:, hc:hc + 1] - acs_r[h:h + 1, :]
                decay = jnp.exp(jnp.where(valid, dm, -jnp.inf))
                wmat = (cb * decay * dt_r[h:h + 1, :]).astype(BF16)
                keep = (lane < SSD_HEAD_DIM) if sub == 0 else (lane >= SSD_HEAD_DIM)
                xh = jnp.where(keep, x_pair, jnp.zeros_like(x_pair))
                part = jnp.dot(wmat, xh, preferred_element_type=F32)
                acc = part if acc is None else acc + part
            cols.append(acc)
        y_diag = jnp.concatenate(cols, axis=1)
        ys.append(y_diag + y_off * e_wide[:, gw * g:gw * (g + 1)])
        xw = (xs_f[:, gw * g:gw * (g + 1)] * w_wide[:, gw * g:gw * (g + 1)]).astype(BF16)
        new = lax.dot_general(bm, xw, (((0,), (0,)), ((), ())), preferred_element_type=F32)
        st_ref[g] = cd_wide[:, gw * g:gw * (g + 1)] * state + new
    return jnp.concatenate(ys, axis=1), xs_f


def _ssd_kernel(xsf_ref, bcf_ref, smf_ref, smtf_ref, xsb_ref, bcb_ref, smb_ref, smtb_ref,
                dtb_r, dtb_c, alog_r, alog_c, dskip_ref, exp_ref, yf_ref, yb_ref, stf, stb):
    @pl.when(pl.program_id(1) == 0)
    def _():
        stf[...] = jnp.zeros_like(stf)
        stb[...] = jnp.zeros_like(stb)

    y, xs_f = _ssd_direction(xsf_ref, bcf_ref, smf_ref, smtf_ref, dtb_r, dtb_c, alog_r, alog_c, exp_ref, stf,
                             0, False)
    yf_ref[...] = (y + dskip_ref[...] * xs_f).astype(BF16)
    y, _ = _ssd_direction(xsb_ref, bcb_ref, smb_ref, smtb_ref, dtb_r, dtb_c, alog_r, alog_c, exp_ref, stb,
                          1, True)
    yb_ref[...] = y.astype(BF16)


def _ssd_call(xs_c, bc_c, small, small_t, dtb_r, dtb_c, alog_r, alog_c, dskip, expand, batch, seq_len):
    t = xs_c.shape[0]
    L = CHUNK
    nc = seq_len // L
    fwd = lambda b, j: (b * nc + j, 0)
    bwd = lambda b, j: (b * nc + (nc - 1 - j), 0)
    fwd_t = lambda b, j: (0, b * nc + j)
    bwd_t = lambda b, j: (0, b * nc + (nc - 1 - j))
    const = lambda b, j: (0, 0)
    nh2 = 2 * SSD_HEADS
    return pl.pallas_call(
        _ssd_kernel,
        out_shape=(jax.ShapeDtypeStruct((t, BRANCH_W), BF16), jax.ShapeDtypeStruct((t, BRANCH_W), BF16)),
        grid=(batch, nc),
        in_specs=[
            pl.BlockSpec((L, BRANCH_W), fwd), pl.BlockSpec((L, 512), fwd),
            pl.BlockSpec((L, LANES), fwd), pl.BlockSpec((nh2, L), fwd_t),
            pl.BlockSpec((L, BRANCH_W), bwd), pl.BlockSpec((L, 512), bwd),
            pl.BlockSpec((L, LANES), bwd), pl.BlockSpec((nh2, L), bwd_t),
            pl.BlockSpec((1, LANES), const), pl.BlockSpec((nh2, 1), const),
            pl.BlockSpec((1, LANES), const), pl.BlockSpec((nh2, 1), const),
            pl.BlockSpec((1, BRANCH_W), const), pl.BlockSpec((2, LANES, BRANCH_W), lambda b, j: (0, 0, 0)),
        ],
        out_specs=(pl.BlockSpec((L, BRANCH_W), fwd), pl.BlockSpec((L, BRANCH_W), bwd)),
        scratch_shapes=[pltpu.VMEM((SSD_GROUPS, SSD_STATE, 512), F32), pltpu.VMEM((SSD_GROUPS, SSD_STATE, 512), F32)],
        compiler_params=_cparams(("arbitrary", "arbitrary")),
        name="ssd_scan",
    )(xs_c, bc_c, small, small_t, xs_c, bc_c, small, small_t, dtb_r, dtb_c, alog_r, alog_c, dskip, expand)


def _swap32(y):
    lane = lax.broadcasted_iota(jnp.int32, y.shape, 1)
    lo = (lane % 64) < 32
    return jnp.where(lo, pltpu.roll(y, 96, 1), pltpu.roll(y, 32, 1))


def _gqa_prep_kernel(q_ref, k_ref, v_ref, cos_ref, sin_ref, qw_ref, kw_ref, qo_ref, ko_ref, vto_ref, *, scale):
    cos = cos_ref[...]
    sin = sin_ref[...]

    def one(x, w, s):
        x = x.astype(F32)
        ms = jnp.mean(x * x, axis=-1, keepdims=True)
        y = x * lax.rsqrt(ms + EPS) * w
        return ((y * cos + _swap32(y) * sin) * s).astype(BF16)

    for h in range(GQA_HEADS):
        sl = slice(LANES * h, LANES * (h + 1))
        qo_ref[:, sl] = one(q_ref[:, sl], qw_ref[...], scale)
    for h in range(GQA_KV_HEADS):
        sl = slice(LANES * h, LANES * (h + 1))
        ko_ref[h] = one(k_ref[:, sl], kw_ref[...], 1.0)
        vto_ref[sl, :] = v_ref[:, sl].astype(F32).T.astype(BF16)


def _gqa_prep_call(big, cos_t, sin_t, qw, kw, seq_len):
    t = big.shape[0]
    tm = 512
    tps = seq_len // tm
    kern = functools.partial(_gqa_prep_kernel, scale=GQA_HEAD_DIM ** -0.5 * math.log2(math.e))
    kvw = GQA_KV_HEADS * GQA_HEAD_DIM
    return pl.pallas_call(
        kern,
        out_shape=(jax.ShapeDtypeStruct((t, BRANCH_W), BF16), jax.ShapeDtypeStruct((GQA_KV_HEADS, t, LANES), BF16),
                   jax.ShapeDtypeStruct((kvw, t), BF16)),
        grid=(t // tm,),
        in_specs=[
            pl.BlockSpec((tm, BRANCH_W), lambda i: (i, C_GQ // BRANCH_W)),
            pl.BlockSpec((tm, kvw), lambda i: (i, C_GK // kvw)),
            pl.BlockSpec((tm, kvw), lambda i: (i, C_GV // kvw)),
            pl.BlockSpec((tm, LANES), lambda i: (i % tps, 0)),
            pl.BlockSpec((tm, LANES), lambda i: (i % tps, 0)),
            pl.BlockSpec((1, LANES), lambda i: (0, 0)),
            pl.BlockSpec((1, LANES), lambda i: (0, 0)),
        ],
        out_specs=(pl.BlockSpec((tm, BRANCH_W), lambda i: (i, 0)),
                   pl.BlockSpec((GQA_KV_HEADS, tm, LANES), lambda i: (0, i, 0)),
                   pl.BlockSpec((kvw, tm), lambda i: (0, i))),
        compiler_params=_cparams(("arbitrary",)),
        name="gqa_prep",
    )(big, big, big, cos_t, sin_t, qw, kw)


def _na_prep_kernel(q_ref, k_ref, v_ref, qw_ref, kw_ref, qo_ref, ko_ref, vo_ref, *, scale):
    def one(x, w, s):
        x = x.astype(F32)
        lane = lax.broadcasted_iota(jnp.int32, x.shape, 1)
        lo = lane < NA_HEAD_DIM
        x2 = x * x
        s_lo = jnp.sum(jnp.where(lo, x2, 0.0), axis=-1, keepdims=True)
        s_hi = jnp.sum(jnp.where(lo, 0.0, x2), axis=-1, keepdims=True)
        ms = jnp.where(lo, s_lo, s_hi) * (1.0 / NA_HEAD_DIM)
        return (x * lax.rsqrt(ms + EPS) * w * s).astype(BF16)

    for hp in range(NA_HEADS // 2):
        sl = slice(LANES * hp, LANES * (hp + 1))
        qo_ref[:, sl] = one(q_ref[:, sl], qw_ref[...], scale)
        ko_ref[hp] = one(k_ref[:, sl], kw_ref[...], 1.0)
        vo_ref[hp] = v_ref[:, sl]


def _na_prep_call(big, qw2, kw2):
    t = big.shape[0]
    tm = 512
    npair = NA_HEADS // 2
    kern = functools.partial(_na_prep_kernel, scale=NA_HEAD_DIM ** -0.5)
    slab = jax.ShapeDtypeStruct((npair, t, LANES), BF16)
    slab_spec = pl.BlockSpec((npair, tm, LANES), lambda i: (0, i, 0))
    return pl.pallas_call(
        kern,
        out_shape=(jax.ShapeDtypeStruct((t, BRANCH_W), BF16), slab, slab),
        grid=(t // tm,),
        in_specs=[
            pl.BlockSpec((tm, BRANCH_W), lambda i: (i, C_NQ // BRANCH_W)),
            pl.BlockSpec((tm, BRANCH_W), lambda i: (i, C_NK // BRANCH_W)),
            pl.BlockSpec((tm, BRANCH_W), lambda i: (i, C_NV // BRANCH_W)),
            pl.BlockSpec((1, LANES), lambda i: (0, 0)),
            pl.BlockSpec((1, LANES), lambda i: (0, 0)),
        ],
        out_specs=(pl.BlockSpec((tm, BRANCH_W), lambda i: (i, 0)), slab_spec, slab_spec),
        compiler_params=_cparams(("arbitrary",)),
        name="na_prep",
    )(big, big, big, qw2, kw2)


FLASH_DEN_ROWS = 16


def _flash_kernel(q_ref, k_ref, vt_ref, o_ref, m_scr, acc, *, tq, tk, grp, nkb, unroll):
    m_scr[...] = jnp.full(m_scr.shape, -jnp.inf, F32)
    acc[...] = jnp.zeros(acc.shape, F32)
    ones_rows = jnp.ones((FLASH_DEN_ROWS, tk), BF16)

    def body(it, carry):
        units = [(j, h) for j in range(unroll) for h in range(grp)]

        def kv_start(j):
            return pl.multiple_of((it * unroll + j) * tk, tk)

        def scores(u):
            j, h = units[u]
            kblk = k_ref[pl.ds(kv_start(j), tk), :]
            qh = q_ref[:, LANES * h:LANES * (h + 1)]
            return lax.dot_general(kblk, qh, (((1,), (1,)), ((), ())), preferred_element_type=F32)

        def softmax(u, st):
            h = units[u][1]
            m_old = m_scr[h]
            m_new = jnp.maximum(m_old, jnp.max(st, axis=0, keepdims=True))
            alpha = jnp.exp2(m_old - m_new)
            p = jnp.exp2(st - m_new)
            m_scr[h] = m_new
            return alpha, p.astype(BF16)

        def accumulate(u, alpha, p):
            j, h = units[u]
            vt = jnp.concatenate([vt_ref[:, pl.ds(kv_start(j), tk)], ones_rows], axis=0)
            acc[h] = alpha * acc[h] + jnp.dot(vt, p, preferred_element_type=F32)

        n = len(units)
        st = {u: scores(u) for u in range(min(2, n))}
        pending = None
        for u in range(n):
            cur = softmax(u, st.pop(u))
            if u + 2 < n:
                st[u + 2] = scores(u + 2)
            if pending is not None:
                accumulate(u - 1, *pending)
            pending = cur
        accumulate(n - 1, *pending)
        return carry

    lax.fori_loop(0, nkb // unroll, body, 0)
    for h in range(grp):
        a = acc[h]
        o = (a[:LANES, :] / a[LANES:LANES + 1, :]).T
        o_ref[:, LANES * h:LANES * (h + 1)] = o.astype(BF16)


def _flash_call(qp, kp, vt, batch, seq_len):
    t = qp.shape[0]
    grp = GQA_HEADS // GQA_KV_HEADS
    tq, tk = 256, 512
    nq = seq_len // tq
    gw = grp * LANES
    nkb = seq_len // tk
    unroll = math.gcd(nkb, 4)
    kern = functools.partial(_flash_kernel, tq=tq, tk=tk, grp=grp, nkb=nkb, unroll=unroll)
    return pl.pallas_call(
        kern,
        out_shape=jax.ShapeDtypeStruct((t, BRANCH_W), BF16),
        grid=(batch, GQA_KV_HEADS, nq),
        in_specs=[
            pl.BlockSpec((tq, gw), lambda b, g, qi: (b * nq + qi, g)),
            pl.BlockSpec((None, seq_len, LANES), lambda b, g, qi: (g, b, 0)),
            pl.BlockSpec((LANES, seq_len), lambda b, g, qi: (g, b)),
        ],
        out_specs=pl.BlockSpec((tq, gw), lambda b, g, qi: (b * nq + qi, g)),
        scratch_shapes=[pltpu.VMEM((grp, 1, tq), F32), pltpu.VMEM((grp, LANES + FLASH_DEN_ROWS, tq), F32)],
        compiler_params=_cparams(("arbitrary", "arbitrary", "arbitrary")),
        name="gqa_flash",
    )(qp, kp, vt)


NA_ROWS_PER_STEP = 4


def _na_key_rows(r):
    kr = r + NA_WIN_ROWS - 1
    return kr + (kr * GRID_W % LANES) // GRID_W


NA_UNITS_PER_STEP = 4


def _na_kernel(q_ref, k_ref, v_ref, tbl_ref, o_ref, *, rows, R, U):
    KR = _na_key_rows(R)
    W = GRID_W
    lane = lax.broadcasted_iota(jnp.int32, (W, LANES), 1)
    head0 = lane < NA_HEAD_DIM
    zero = jnp.zeros((W, LANES), BF16)

    def row0(u):
        return (pl.program_id(2) * U + u) * R

    def key_row0(u):
        return jnp.clip(row0(u) - NA_WIN_ROWS // 2, 0, rows - KR)

    def key_slice(u):
        return pl.ds(pl.multiple_of(key_row0(u) * W, W), KR * W)

    def scores(u):
        kblk = k_ref[key_slice(u), :]
        parts = []
        for qr in range(R):
            qrow = q_ref[W * (R * u + qr):W * (R * u + qr + 1), :]
            parts.append(jnp.where(head0, qrow, zero))
            parts.append(jnp.where(head0, zero, qrow))
        qcat = jnp.concatenate(parts, axis=0)
        return lax.dot_general(kblk, qcat, (((1,), (1,)), ((), ())), preferred_element_type=F32)

    def softmax(u, st):
        r0, ks = row0(u), key_row0(u)
        blocks = []
        for a in range(KR):
            rowb = []
            for qr in range(R):
                krow = ks + a
                r = r0 + qr
                rs = jnp.clip(r - NA_WIN_ROWS // 2, 0, rows - NA_WIN_ROWS)
                ok = jnp.logical_and(krow >= rs, krow < rs + NA_WIN_ROWS)
                idx = jnp.where(ok, krow - r + (NA_WIN_ROWS - 1), 2 * NA_WIN_ROWS - 1)
                rowb.append(st[W * a:W * (a + 1), LANES * qr:LANES * (qr + 1)] + tbl_ref[idx])
            blocks.append(jnp.concatenate(rowb, axis=1))
        s = jnp.concatenate(blocks, axis=0)
        m = jnp.max(s, axis=0, keepdims=True)
        return jnp.exp(s - m).astype(BF16)

    def values(u, p):
        vblk = v_ref[key_slice(u), :]
        vaug = jnp.concatenate([vblk, jnp.ones_like(vblk)], axis=1)
        o = lax.dot_general(p, vaug, (((0,), (0,)), ((), ())), preferred_element_type=F32)
        o = o[:, :LANES] / o[:, LANES:]
        for qr in range(R):
            top = o[2 * W * qr:2 * W * qr + W, :]
            bot = o[2 * W * qr + W:2 * W * (qr + 1), :]
            o_ref[W * (R * u + qr):W * (R * u + qr + 1), :] = jnp.where(head0, top, bot).astype(BF16)

    st = {u: scores(u) for u in range(min(2, U))}
    pending = None
    for u in range(U):
        p = softmax(u, st.pop(u))
        if u + 2 < U:
            st[u + 2] = scores(u + 2)
        if pending is not None:
            values(u - 1, pending)
        pending = p
    values(U - 1, pending)


def _na_call(qn, kn, vn, tbl, batch, seq_len):
    t = qn.shape[0]
    rows = seq_len // GRID_W
    R, U = NA_ROWS_PER_STEP, NA_UNITS_PER_STEP
    nst = rows // (R * U)
    kern = functools.partial(_na_kernel, rows=rows, R=R, U=U)
    return pl.pallas_call(
        kern,
        out_shape=jax.ShapeDtypeStruct((t, BRANCH_W), BF16),
        grid=(batch, NA_HEADS // 2, nst),
        in_specs=[
            pl.BlockSpec((U * R * GRID_W, LANES), lambda b, hp, i: (b * nst + i, hp)),
            pl.BlockSpec((None, seq_len, LANES), lambda b, hp, i: (hp, b, 0)),
            pl.BlockSpec((None, seq_len, LANES), lambda b, hp, i: (hp, b, 0)),
            pl.BlockSpec((None, 2 * NA_WIN_ROWS, GRID_W, LANES), lambda b, hp, i: (hp, 0, 0, 0)),
        ],
        out_specs=pl.BlockSpec((U * R * GRID_W, LANES), lambda b, hp, i: (b * nst + i, hp)),
        compiler_params=_cparams(("arbitrary", "arbitrary", "arbitrary")),
        name="na_attn",
    )(qn, kn, vn, tbl)


def _mlstm_gates(sm_ref, smt_ref, gb_r, gb_c, backward):
    L = CHUNK
    g_c = sm_ref[...] + gb_r[...]
    g_r = smt_ref[...] + gb_c[...]
    m_c = _tri(L, lower=not backward).astype(BF16)
    m_r = _tri(L, lower=backward).astype(BF16)
    bcs_c = _dot_exact_lhs(m_c, -_softplus(-g_c))
    bcs_r = _dot_exact_rhs(-_softplus(-g_r), m_r)
    return g_c, g_r, bcs_c, bcs_r, _tri(L, lower=not backward)


def _mlstm_unit(refs, gates, d, h, backward):
    q_ref, k_ref, v_ref, c_scr, n_scr, m_scr, o_ref = refs
    g_c, g_r, bcs_c_all, bcs_r_all, valid = gates
    L = CHUNK
    ii = 2 * ML_HEADS * d + h
    fi = ii + ML_HEADS
    sl = slice(ML_HEAD_DIM * h, ML_HEAD_DIM * (h + 1))
    t = {}

    def stage_a():
        t["qh"] = q_ref[:, sl]
        t["kh"] = k_ref[:, sl] * (ML_HEAD_DIM ** -0.5)
        t["sc"] = lax.dot_general(t["qh"], t["kh"], (((1,), (1,)), ((), ())), preferred_element_type=F32)
        t["c_prev"] = c_scr[h]
        t["qc"] = jnp.dot(t["qh"], t["c_prev"].astype(BF16), preferred_element_type=F32)

    def stage_b():
        li_c = g_c[:, S_MG + ii:S_MG + ii + 1]
        li_r = g_r[ii:ii + 1, :]
        bcs_c = bcs_c_all[:, S_MG + fi:S_MG + fi + 1]
        bcs_r = bcs_r_all[fi:fi + 1, :]
        b_end = bcs_c[0:1, :] if backward else bcs_c[L - 1:L, :]
        dm = jnp.where(valid, bcs_c - (bcs_r - li_r), -jnp.inf)
        m_prev = m_scr[h, 0:1, 0:1]
        inter = bcs_c + m_prev
        m_t = jnp.maximum(inter, jnp.max(dm, axis=1, keepdims=True))
        w_intra = jnp.exp(dm - m_t) * t.pop("sc")
        t["w_inter"] = jnp.exp(inter - m_t)
        t["n_prev"] = n_scr[h, 0:1, :]
        den = (jnp.sum(w_intra, axis=1, keepdims=True)
               + t["w_inter"] * jnp.sum(t["qh"].astype(F32) * t["n_prev"], axis=1, keepdims=True))
        t["den"] = jnp.maximum(jnp.abs(den), jnp.exp(-m_t))
        t["w_intra"] = w_intra.astype(BF16)
        g_end_r = b_end - bcs_r + li_r
        m_loc = jnp.max(g_end_r, axis=1, keepdims=True)
        w_end_c = jnp.exp(b_end - bcs_c + li_c - m_loc)
        kw = t.pop("kh").astype(F32) * w_end_c
        t["n_loc"] = jnp.sum(kw, axis=0, keepdims=True)
        t["kw"] = kw.astype(BF16)
        m_new = jnp.maximum(b_end + m_prev, m_loc)
        t["a_old"] = jnp.exp(b_end + m_prev - m_new)
        t["a_loc"] = jnp.exp(m_loc - m_new)
        t["m_new"] = m_new

    def stage_c():
        vh = v_ref[:, sl]
        t["wv"] = jnp.dot(t.pop("w_intra"), vh, preferred_element_type=F32)
        t["c_loc"] = lax.dot_general(t.pop("kw"), vh, (((0,), (0,)), ((), ())), preferred_element_type=F32)

    def stage_d():
        num = t["wv"] + t["w_inter"] * t["qc"]
        o_ref[:, sl] = (num / t["den"]).astype(BF16)
        c_scr[h] = t["a_old"] * t["c_prev"] + t["a_loc"] * t["c_loc"]
        n_scr[h] = jnp.broadcast_to(t["a_old"] * t["n_prev"] + t["a_loc"] * t["n_loc"], n_scr.shape[1:])
        m_scr[h] = jnp.broadcast_to(t["m_new"], m_scr.shape[1:])

    return stage_a, stage_b, stage_c, stage_d


def _mlstm_kernel(qf, kf, vf, smf, smtf, qb, kb, vb, smb, smtb, gb_r, gb_c, hf_ref, hb_ref,
                  cf, nf, mf, cb, nb, mb):
    @pl.when(pl.program_id(1) == 0)
    def _():
        for r in (cf, nf, mf, cb, nb, mb):
            r[...] = jnp.zeros_like(r)

    gates_f = _mlstm_gates(smf, smtf, gb_r, gb_c, False)
    gates_b = _mlstm_gates(smb, smtb, gb_r, gb_c, True)
    units = []
    for h in range(ML_HEADS):
        units.append(_mlstm_unit((qf, kf, vf, cf, nf, mf, hf_ref), gates_f, 0, h, False))
        units.append(_mlstm_unit((qb, kb, vb, cb, nb, mb, hb_ref), gates_b, 1, h, True))
    n = len(units)
    for u in range(min(2, n)):
        units[u][0]()
    for u in range(n):
        units[u][1]()
        if u + 2 < n:
            units[u + 2][0]()
        units[u][2]()
        if u >= 1:
            units[u - 1][3]()
    units[n - 1][3]()


def _mlstm_call(big, small, small_t, gb_r, gb_c, batch, seq_len):
    t = big.shape[0]
    L = CHUNK
    nc = seq_len // L
    cq, ck, cv = C_MQ // BRANCH_W, C_MK // BRANCH_W, C_MV // BRANCH_W
    f = lambda c: (lambda b, j: (b * nc + j, c))
    r = lambda c: (lambda b, j: (b * nc + (nc - 1 - j), c))
    const = lambda b, j: (0, 0)
    state = [pltpu.VMEM((ML_HEADS, ML_HEAD_DIM, ML_HEAD_DIM), F32), pltpu.VMEM((ML_HEADS, 8, ML_HEAD_DIM), F32),
             pltpu.VMEM((ML_HEADS, 8, LANES), F32)]
    return pl.pallas_call(
        _mlstm_kernel,
        out_shape=(jax.ShapeDtypeStruct((t, BRANCH_W), BF16), jax.ShapeDtypeStruct((t, BRANCH_W), BF16)),
        grid=(batch, nc),
        in_specs=[
            pl.BlockSpec((L, BRANCH_W), f(cq)), pl.BlockSpec((L, BRANCH_W), f(ck)), pl.BlockSpec((L, BRANCH_W), f(cv)),
            pl.BlockSpec((L, LANES), f(0)), pl.BlockSpec((16, L), lambda b, j: (S_MG // 16, b * nc + j)),
            pl.BlockSpec((L, BRANCH_W), r(cq)), pl.BlockSpec((L, BRANCH_W), r(ck)), pl.BlockSpec((L, BRANCH_W), r(cv)),
            pl.BlockSpec((L, LANES), r(0)), pl.BlockSpec((16, L), lambda b, j: (S_MG // 16, b * nc + (nc - 1 - j))),
            pl.BlockSpec((1, LANES), const), pl.BlockSpec((16, 1), const),
        ],
        out_specs=(pl.BlockSpec((L, BRANCH_W), f(0)), pl.BlockSpec((L, BRANCH_W), r(0))),
        scratch_shapes=state + state,
        compiler_params=_cparams(("arbitrary", "arbitrary")),
        name="mlstm_scan",
    )(big, big, big, small, small_t, big, big, big, small, small_t, gb_r, gb_c)


def _merge_kernel(yf_ref, yb_ref, z_ref, og_ref, on_ref, hf_ref, hb_ref, mo_ref, gate_ref, snw_ref, mnw_ref,
                  wb_ref, o_ref):
    y = (yf_ref[...].astype(F32) + yb_ref[...].astype(F32)) * _silu(z_ref[...].astype(F32))
    ms = jnp.mean(y * y, axis=-1, keepdims=True)
    y0 = (y * lax.rsqrt(ms + EPS) * snw_ref[...]).astype(BF16)
    parts = []
    for h in range(ML_HEADS):
        sl = slice(ML_HEAD_DIM * h, ML_HEAD_DIM * (h + 1))
        hh = hf_ref[:, sl].astype(F32) + hb_ref[:, sl].astype(F32)
        ms = jnp.mean(hh * hh, axis=-1, keepdims=True)
        parts.append(hh * lax.rsqrt(ms + EPS) * mnw_ref[:, sl])
    y3 = (jax.nn.sigmoid(mo_ref[...].astype(F32)) * jnp.concatenate(parts, axis=1)).astype(BF16)
    acc = None
    for i, yi in enumerate((y0, og_ref[...], on_ref[...], y3)):
        g = gate_ref[:, D_MODEL * i:D_MODEL * (i + 1)].astype(F32)
        term = g * jnp.dot(yi, wb_ref[i], preferred_element_type=F32)
        acc = term if acc is None else acc + term
    o_ref[...] = acc.astype(BF16)


def _merge_call(yf, yb, og, on, hf, hb, big, snw, mnw, wb):
    t = big.shape[0]
    tm = 256
    row = lambda i: (i, 0)
    const2 = lambda i: (0, 0)
    return pl.pallas_call(
        _merge_kernel,
        out_shape=jax.ShapeDtypeStruct((t, D_MODEL), BF16),
        grid=(t // tm,),
        in_specs=[
            pl.BlockSpec((tm, BRANCH_W), row), pl.BlockSpec((tm, BRANCH_W), row),
            pl.BlockSpec((tm, BRANCH_W), lambda i: (i, C_Z // BRANCH_W)),
            pl.BlockSpec((tm, BRANCH_W), row), pl.BlockSpec((tm, BRANCH_W), row),
            pl.BlockSpec((tm, BRANCH_W), row), pl.BlockSpec((tm, BRANCH_W), row),
            pl.BlockSpec((tm, BRANCH_W), lambda i: (i, C_MO // BRANCH_W)),
            pl.BlockSpec((tm, N_BRANCH * D_MODEL), lambda i: (i, 0)),
            pl.BlockSpec((1, BRANCH_W), const2), pl.BlockSpec((1, BRANCH_W), const2),
            pl.BlockSpec((N_BRANCH, BRANCH_W, D_MODEL), lambda i: (0, 0, 0), pipeline_mode=pl.Buffered(1)),
        ],
        out_specs=pl.BlockSpec((tm, D_MODEL), row),
        compiler_params=_cparams(("arbitrary",)),
        name="merge",
    )(yf, yb, big, og, on, hf, hb, big, big, snw, mnw, wb)


def _resproj_kernel(a_ref, w_ref, x_ref, mod_ref, o_ref, *, mod_row):
    acc = jnp.dot(a_ref[...], w_ref[...], preferred_element_type=F32)
    o_ref[...] = x_ref[...] + mod_ref[mod_row:mod_row + 1, :] * acc


def _resproj_call(a, w, x, mod, mod_row, seq_len, tm, tn):
    t, k = a.shape
    n = w.shape[1]
    tiles_per_seq = seq_len // tm
    kern = functools.partial(_resproj_kernel, mod_row=mod_row)
    return pl.pallas_call(
        kern,
        out_shape=jax.ShapeDtypeStruct((t, n), F32),
        grid=(n // tn, t // tm),
        in_specs=[
            pl.BlockSpec((tm, k), lambda j, i: (i, 0)),
            pl.BlockSpec((k, tn), lambda j, i: (0, j)),
            pl.BlockSpec((tm, tn), lambda j, i: (i, j)),
            pl.BlockSpec((None, 6, tn), lambda j, i: (i // tiles_per_seq, 0, j)),
        ],
        out_specs=pl.BlockSpec((tm, tn), lambda j, i: (i, j)),
        compiler_params=_cparams(("arbitrary", "arbitrary")),
        name="resproj",
    )(a, w, x, mod)


def _ffn_in_kernel(x_ref, mod_ref, nw_ref, wu_ref, wg_ref, o_ref, h_scr, *, tm, rc):
    @pl.when(pl.program_id(1) == 0)
    def _():
        nw = nw_ref[...]
        sh = mod_ref[3:4, :]
        sc = 1.0 + mod_ref[4:5, :]

        def body(r, carry):
            r0 = pl.multiple_of(r * rc, rc)
            h_scr[pl.ds(r0, rc), :] = _norm_mod_rows(x_ref, nw, sc, sh, r0, rc).astype(BF16)
            return carry

        lax.fori_loop(0, tm // rc, body, 0)

    h = h_scr[...]
    up = jnp.dot(h, wu_ref[...], preferred_element_type=F32)
    gate = jnp.dot(h, wg_ref[...], preferred_element_type=F32)
    o_ref[...] = (_silu(gate) * up).astype(BF16)


def _ffn_in_call(x, mod, nw, w_ffn_in, seq_len):
    t = x.shape[0]
    tm, rc = 1024, 128
    tn = w_ffn_in.shape[-1]
    nt = FFN_HIDDEN // tn
    tiles_per_seq = seq_len // tm
    kern = functools.partial(_ffn_in_kernel, tm=tm, rc=rc)
    return pl.pallas_call(
        kern,
        out_shape=jax.ShapeDtypeStruct((t, FFN_HIDDEN), BF16),
        grid=(t // tm, nt),
        in_specs=[
            pl.BlockSpec((tm, D_MODEL), lambda i, j: (i, 0)),
            pl.BlockSpec((None, 6, D_MODEL), lambda i, j: (i // tiles_per_seq, 0, 0)),
            pl.BlockSpec((1, D_MODEL), lambda i, j: (0, 0)),
            pl.BlockSpec((None, D_MODEL, tn), lambda i, j: (j, 0, 0)),
            pl.BlockSpec((None, D_MODEL, tn), lambda i, j: (nt + j, 0, 0)),
        ],
        out_specs=pl.BlockSpec((tm, tn), lambda i, j: (i, j)),
        scratch_shapes=[pltpu.VMEM((tm, D_MODEL), BF16)],
        compiler_params=_cparams(("arbitrary", "arbitrary")),
        name="ffn_in",
    )(x, mod, nw, w_ffn_in, w_ffn_in)


def _rope_tables(seq_len):
    pos = np.arange(seq_len)
    rows, cols = pos // GRID_W, pos % GRID_W
    d = GQA_HEAD_DIM // 2
    inv = np.asarray(ROPE_THETA ** (-np.arange(0, d, 2, dtype=np.float32) / d), dtype=np.float32)

    def cs(p):
        ang = p[:, None].astype(np.float32) * inv[None]
        return np.cos(ang), np.sin(ang)

    cr, sr = cs(rows)
    cc, sc = cs(cols)
    cos = np.concatenate([cr, cr, cc, cc], axis=1).astype(np.float32)
    sin = np.concatenate([-sr, sr, -sc, sc], axis=1).astype(np.float32)
    return jnp.asarray(cos), jnp.asarray(sin)


def _na_bias_table(rpb):
    c = np.arange(GRID_W)[:, None]
    j = np.arange(GRID_W)[None, :]
    start = np.clip(j - NA_WIN_COLS // 2, 0, GRID_W - NA_WIN_COLS)
    ok = (c >= start) & (c < start + NA_WIN_COLS)
    dc = np.clip(c - j + (NA_WIN_COLS - 1), 0, 2 * NA_WIN_COLS - 2)
    t = rpb[:, :, dc]
    t = jnp.where(jnp.asarray(ok)[None, None], t, NEG)
    t = t.reshape(NA_HEADS // 2, 2, 2 * NA_WIN_ROWS - 1, GRID_W, GRID_W)
    t = t.transpose(0, 2, 3, 1, 4).reshape(NA_HEADS // 2, 2 * NA_WIN_ROWS - 1, GRID_W, LANES)
    pad = jnp.full((NA_HEADS // 2, 1, GRID_W, LANES), NEG, F32)
    return jnp.concatenate([t, pad], axis=1)


def _lane_row(v, offset):
    return jnp.zeros((1, LANES), F32).at[0, offset:offset + v.shape[0]].set(v)


INPROJ_TN = 1024
FFN_TN = 512


def _tile_major(w, tn):
    k, n = w.shape
    return w.reshape(k, n // tn, tn).transpose(1, 0, 2)


def _layer_weights(l, w_in, w_bgate, b_bgate):
    o = np.cumsum((0, 1024, 1536, 32, 1024, 256, 256, 1024, 1024, 1024, 1024, 1024, 1024, 1024, 16))
    cols = lambda w: [w[:, o[i]:o[i + 1]] for i in range(14)]
    z, xbc, _, gq, gk, gv, nq, nk, nv, mq, mk, mv, mo, _ = cols(w_in[l].astype(BF16))
    w_big = jnp.concatenate([w_bgate[l].astype(BF16), z, xbc[:, :1024], gq, nq, nk, nv, mq, mk, mv, mo,
                             xbc[:, 1024:], gk, gv], axis=1)
    w_big = _tile_major(w_big, INPROJ_TN)
    b_big = jnp.concatenate([b_bgate[l], jnp.zeros((BIG_COLS - N_BRANCH * D_MODEL,), F32)])[None, :]
    dt, mg = w_in[l][:, o[2]:o[3]], w_in[l][:, o[13]:o[14]]
    ws = jnp.concatenate([dt, mg, jnp.zeros((D_MODEL, LANES - 48), F32)], axis=1)
    ws_hi = ws.astype(BF16)
    ws_lo = (ws - ws_hi.astype(F32)).astype(BF16)
    return w_big, b_big, ws_hi, ws_lo


def _trunk_group(x, mods, seq_len, batch, consts, layers):
    for l in range(DEPTH):
        lw = layers[l]
        mod = mods[l]
        big, small = _inproj_call(x, mod, lw["norm1"], lw["w_big"], lw["b_big"], lw["ws_hi"], lw["ws_lo"], seq_len)
        small_t = small.T
        xs_c, bc_c = _conv_call(big, lw["conv_w1"], lw["conv_b1"], lw["conv_w2"], lw["conv_b2"], seq_len)
        yf, yb = _ssd_call(xs_c, bc_c, small, small_t, lw["dtb_r"], lw["dtb_c"], lw["alog_r"], lw["alog_c"],
                           lw["dskip"], consts["expand"], batch, seq_len)
        qp, kp, vt = _gqa_prep_call(big, consts["cos"][seq_len], consts["sin"][seq_len], lw["gqa_qw"], lw["gqa_kw"],
                                    seq_len)
        og = _flash_call(qp, kp, vt, batch, seq_len)
        qn, kn, vn = _na_prep_call(big, lw["na_qw"], lw["na_kw"])
        on = _na_call(qn, kn, vn, lw["na_tbl"], batch, seq_len)
        hf, hb = _mlstm_call(big, small, small_t, lw["ml_gb_r"], lw["ml_gb_c"], batch, seq_len)
        merged = _merge_call(yf, yb, og, on, hf, hb, big, lw["ssd_nw"], lw["ml_nw"], lw["wb"])
        x = _resproj_call(merged, lw["w_o"], x, mod, 2, seq_len, 256, D_MODEL)
        act = _ffn_in_call(x, mod, lw["norm2"], lw["w_ffn_in"], seq_len)
        x = _resproj_call(act, lw["w_ffn_out"], x, mod, 5, seq_len, 512, 1024)
    return x


def kernel(x_prompt, x_sample, c_prompt, c_sample, w_ada, b_ada, norm1_w, w_in, w_bgate, b_bgate, ssd_conv_w,
           ssd_conv_b, ssd_dt_bias, ssd_a_log, ssd_d, ssd_norm_w, gqa_q_norm, gqa_k_norm, na_q_norm, na_k_norm,
           na_rpb, ml_gate_bias, ml_out_norm, w_branch_out, w_o, norm2_w, w_ffn_in, w_ffn_out):
    bp, sp, _ = x_prompt.shape
    bs, ss, _ = x_sample.shape
    nseq = bp + bs
    rows = -(-nseq // 16) * 16
    c_all = jnp.concatenate([c_prompt, c_sample, jnp.zeros((rows - nseq, D_MODEL), F32)], axis=0)
    mods = _ada_call(c_all, w_ada.astype(BF16), b_ada.reshape(DEPTH, 1, 6 * D_MODEL))
    mods = mods.reshape(DEPTH, rows, 6, D_MODEL)

    expand = np.zeros((2, LANES, BRANCH_W), np.float32)
    for d in range(2):
        for h in range(SSD_HEADS):
            expand[d, SSD_HEADS * d + h, SSD_HEAD_DIM * h:SSD_HEAD_DIM * (h + 1)] = 1.0
    consts = {"expand": jnp.asarray(expand, BF16), "cos": {}, "sin": {}}
    for s in {sp, ss}:
        consts["cos"][s], consts["sin"][s] = _rope_tables(s)

    layers = []
    for l in range(DEPTH):
        w_big, b_big, ws_hi, ws_lo = _layer_weights(l, w_in, w_bgate, b_bgate)
        layers.append({
            "norm1": norm1_w[l][None, :], "norm2": norm2_w[l][None, :],
            "w_big": w_big, "b_big": b_big, "ws_hi": ws_hi, "ws_lo": ws_lo,
            "conv_w1": ssd_conv_w[l][:, :BRANCH_W], "conv_b1": ssd_conv_b[l][None, :BRANCH_W],
            "conv_w2": ssd_conv_w[l][:, BRANCH_W:], "conv_b2": ssd_conv_b[l][None, BRANCH_W:],
            "dtb_r": _lane_row(ssd_dt_bias[l].reshape(-1), S_DT), "dtb_c": ssd_dt_bias[l].reshape(2 * SSD_HEADS, 1),
            "alog_r": _lane_row(ssd_a_log[l].reshape(-1), S_DT), "alog_c": ssd_a_log[l].reshape(2 * SSD_HEADS, 1),
            "dskip": jnp.repeat(ssd_d[l], SSD_HEAD_DIM)[None, :],
            "ssd_nw": ssd_norm_w[l][None, :],
            "gqa_qw": gqa_q_norm[l][None, :], "gqa_kw": gqa_k_norm[l][None, :],
            "na_qw": jnp.tile(na_q_norm[l], 2)[None, :], "na_kw": jnp.tile(na_k_norm[l], 2)[None, :],
            "na_tbl": _na_bias_table(na_rpb[l]),
            "ml_gb_r": _lane_row(ml_gate_bias[l].reshape(-1), S_MG), "ml_gb_c": ml_gate_bias[l].reshape(16, 1),
            "ml_nw": ml_out_norm[l][None, :],
            "wb": w_branch_out[l].astype(BF16), "w_o": w_o[l].astype(BF16),
            "w_ffn_in": _tile_major(w_ffn_in[l].astype(BF16), FFN_TN), "w_ffn_out": w_ffn_out[l].astype(BF16),
        })

    y_p = _trunk_group(x_prompt.reshape(bp * sp, D_MODEL), mods[:, :bp], sp, bp, consts, layers)
    y_s = _trunk_group(x_sample.reshape(bs * ss, D_MODEL), mods[:, bp:bp + bs], ss, bs, consts, layers)
    return y_p.reshape(bp, sp, D_MODEL), y_s.reshape(bs, ss, D_MODEL)
```

```python
import functools
import math

import numpy as np
import jax
import jax.numpy as jnp
from jax import lax
from jax.experimental import pallas as pl
from jax.experimental.pallas import tpu as pltpu

F32 = jnp.float32
BF16 = jnp.bfloat16

D_MODEL = 2048
DEPTH = 4
GRID_W = 64
EPS = 1e-6
N_BRANCH = 4
BRANCH_W = 1024
SSD_HEAD_DIM = 64
SSD_HEADS = 16
SSD_GROUPS = 2
SSD_STATE = 128
SSD_CONV = 5
CHUNK = 128
GQA_HEAD_DIM = 128
GQA_HEADS = 8
GQA_KV_HEADS = 2
ROPE_THETA = 10000.0
NA_HEAD_DIM = 64
NA_HEADS = 16
NA_WIN_ROWS = 8
NA_WIN_COLS = 16
ML_HEADS = 4
ML_HEAD_DIM = 256
FFN_HIDDEN = 5632

VMEM_LIMIT_BYTES = 56 * 1024 * 1024
LANES = 128

C_GATE = 0
C_Z = 8192
C_XS = 9216
C_GQ = 10240
C_NQ = 11264
C_NK = 12288
C_NV = 13312
C_MQ = 14336
C_MK = 15360
C_MV = 16384
C_MO = 17408
C_BM = 18432
C_GK = 18944
C_GV = 19200
BIG_COLS = 19456
S_DT = 0
S_MG = 32
NEG = -1e30


def _cparams(sem):
    return pltpu.CompilerParams(dimension_semantics=sem, vmem_limit_bytes=VMEM_LIMIT_BYTES)


def _silu(x):
    return x * jax.nn.sigmoid(x)


def _softplus(x):
    return jnp.maximum(x, 0.0) + jnp.log1p(jnp.exp(-jnp.abs(x)))


def _split3(x):
    x1 = x.astype(BF16)
    r1 = x - x1.astype(F32)
    x2 = r1.astype(BF16)
    x3 = (r1 - x2.astype(F32)).astype(BF16)
    return x1, x2, x3


def _dot_exact_rhs(x, m):
    a, b, c = _split3(x)
    d = lambda t: jnp.dot(t, m, preferred_element_type=F32)
    return d(a) + d(b) + d(c)


def _dot_exact_lhs(m, x):
    a, b, c = _split3(x)
    d = lambda t: jnp.dot(m, t, preferred_element_type=F32)
    return d(a) + d(b) + d(c)


def _tri(n, lower):
    r = lax.broadcasted_iota(jnp.int32, (n, n), 0)
    c = lax.broadcasted_iota(jnp.int32, (n, n), 1)
    return (c <= r) if lower else (c >= r)


def _ada_kernel(c_ref, w_ref, b_ref, o_ref):
    a = _silu(c_ref[...]).astype(BF16)
    o_ref[...] = jnp.dot(a, w_ref[...], preferred_element_type=F32) + b_ref[...]


def _ada_call(c_pad, w_ada, b_ada):
    rows = c_pad.shape[0]
    tn = 2048
    nt = w_ada.shape[-1] // tn
    return pl.pallas_call(
        _ada_kernel,
        out_shape=jax.ShapeDtypeStruct((DEPTH, rows, w_ada.shape[-1]), F32),
        grid=(DEPTH, nt),
        in_specs=[
            pl.BlockSpec((rows, D_MODEL), lambda l, j: (0, 0)),
            pl.BlockSpec((None, D_MODEL, tn), lambda l, j: (l, 0, j)),
            pl.BlockSpec((None, 1, tn), lambda l, j: (l, 0, j)),
        ],
        out_specs=pl.BlockSpec((None, rows, tn), lambda l, j: (l, 0, j)),
        compiler_params=_cparams(("arbitrary", "arbitrary")),
        name="ada_mod",
    )(c_pad, w_ada, b_ada)


def _norm_mod_rows(x_ref, nw, sc, sh, r0, rc):
    xf = x_ref[pl.ds(r0, rc), :]
    ms = jnp.mean(xf * xf, axis=-1, keepdims=True)
    return xf * lax.rsqrt(ms + EPS) * nw * sc + sh


def _inproj_kernel(x_ref, mod_ref, nw_ref, w_ref, b_ref, wsh_ref, wsl_ref, big_ref, small_ref, h_scr,
                   *, tm, rc, n_gate_tiles):
    j = pl.program_id(1)

    @pl.when(j == 0)
    def _():
        nw = nw_ref[...]
        sh = mod_ref[0:1, :]
        sc = 1.0 + mod_ref[1:2, :]

        def body(r, carry):
            r0 = pl.multiple_of(r * rc, rc)
            hf = _norm_mod_rows(x_ref, nw, sc, sh, r0, rc)
            hb = hf.astype(BF16)
            h_scr[pl.ds(r0, rc), :] = hb
            lo = (hf - hb.astype(F32)).astype(BF16)
            wh = wsh_ref[...]
            sm = (jnp.dot(hb, wh, preferred_element_type=F32)
                  + jnp.dot(lo, wh, preferred_element_type=F32)
                  + jnp.dot(hb, wsl_ref[...], preferred_element_type=F32))
            small_ref[pl.ds(r0, rc), :] = sm
            return carry

        lax.fori_loop(0, tm // rc, body, 0)

    acc = jnp.dot(h_scr[...], w_ref[...], preferred_element_type=F32) + b_ref[...]
    big_ref[...] = jnp.where(j < n_gate_tiles, jax.nn.sigmoid(acc), acc).astype(BF16)


def _inproj_call(x, mod, nw, w_big, b_big, ws_hi, ws_lo, seq_len):
    t = x.shape[0]
    tm, rc = 1024, 128
    tn = INPROJ_TN
    tiles_per_seq = seq_len // tm
    kern = functools.partial(_inproj_kernel, tm=tm, rc=rc, n_gate_tiles=(N_BRANCH * D_MODEL) // tn)
    return pl.pallas_call(
        kern,
        out_shape=(jax.ShapeDtypeStruct((t, BIG_COLS), BF16), jax.ShapeDtypeStruct((t, LANES), F32)),
        grid=(t // tm, BIG_COLS // tn),
        in_specs=[
            pl.BlockSpec((tm, D_MODEL), lambda i, j: (i, 0)),
            pl.BlockSpec((None, 6, D_MODEL), lambda i, j: (i // tiles_per_seq, 0, 0)),
            pl.BlockSpec((1, D_MODEL), lambda i, j: (0, 0)),
            pl.BlockSpec((D_MODEL, tn), lambda i, j: (0, j)),
            pl.BlockSpec((1, tn), lambda i, j: (0, j)),
            pl.BlockSpec((D_MODEL, LANES), lambda i, j: (0, 0)),
            pl.BlockSpec((D_MODEL, LANES), lambda i, j: (0, 0)),
        ],
        out_specs=(pl.BlockSpec((tm, tn), lambda i, j: (i, j)),
                   pl.BlockSpec((tm, LANES), lambda i, j: (i, 0))),
        scratch_shapes=[pltpu.VMEM((tm, D_MODEL), BF16)],
        compiler_params=_cparams(("arbitrary", "arbitrary")),
        name="inproj",
    )(x, mod, nw, w_big, b_big, ws_hi, ws_lo)


HALO = 16


def _conv_kernel(xs_ref, xsp_ref, xsn_ref, bc_ref, bcp_ref, bcn_ref, w1_ref, b1_ref, w2_ref, b2_ref,
                 oxs_ref, obc_ref, ext1, ext2, *, tm, rc, tiles_per_seq):
    i = pl.program_id(0)
    first = (i % tiles_per_seq) == 0
    last = (i % tiles_per_seq) == (tiles_per_seq - 1)
    pad = SSD_CONV // 2
    for main, prv, nxt, w_ref, b_ref, out, ext in (
            (xs_ref, xsp_ref, xsn_ref, w1_ref, b1_ref, oxs_ref, ext1),
            (bc_ref, bcp_ref, bcn_ref, w2_ref, b2_ref, obc_ref, ext2)):
        ext[HALO:HALO + tm, :] = main[...].astype(F32)
        ext[0:HALO, :] = jnp.where(first, 0.0, prv[...].astype(F32))
        ext[HALO + tm:2 * HALO + tm, :] = jnp.where(last, 0.0, nxt[...].astype(F32))
        for c in range(tm // rc):
            acc = jnp.zeros((rc, ext.shape[1]), F32) + b_ref[...]
            for k in range(SSD_CONV):
                s = HALO + c * rc + k - pad
                acc = acc + w_ref[k:k + 1, :] * ext[s:s + rc, :]
            out[c * rc:(c + 1) * rc, :] = _silu(acc).astype(BF16)


def _conv_call(big, w1, b1, w2, b2, seq_len):
    t = big.shape[0]
    tm, rc = 512, 64
    tiles_per_seq = seq_len // tm
    nb = t // HALO
    hb = tm // HALO
    kern = functools.partial(_conv_kernel, tm=tm, rc=rc, tiles_per_seq=tiles_per_seq)
    prev_map = lambda cb: (lambda i: (jnp.maximum(i * hb - 1, 0), cb))
    next_map = lambda cb: (lambda i: (jnp.minimum((i + 1) * hb, nb - 1), cb))
    cxs = C_XS // BRANCH_W
    cbc = C_BM // 512
    return pl.pallas_call(
        kern,
        out_shape=(jax.ShapeDtypeStruct((t, BRANCH_W), BF16), jax.ShapeDtypeStruct((t, 512), BF16)),
        grid=(t // tm,),
        in_specs=[
            pl.BlockSpec((tm, BRANCH_W), lambda i: (i, cxs)),
            pl.BlockSpec((HALO, BRANCH_W), prev_map(cxs)),
            pl.BlockSpec((HALO, BRANCH_W), next_map(cxs)),
            pl.BlockSpec((tm, 512), lambda i: (i, cbc)),
            pl.BlockSpec((HALO, 512), prev_map(cbc)),
            pl.BlockSpec((HALO, 512), next_map(cbc)),
            pl.BlockSpec((SSD_CONV, BRANCH_W), lambda i: (0, 0)),
            pl.BlockSpec((1, BRANCH_W), lambda i: (0, 0)),
            pl.BlockSpec((SSD_CONV, 512), lambda i: (0, 0)),
            pl.BlockSpec((1, 512), lambda i: (0, 0)),
        ],
        out_specs=(pl.BlockSpec((tm, BRANCH_W), lambda i: (i, 0)),
                   pl.BlockSpec((tm, 512), lambda i: (i, 0))),
        scratch_shapes=[pltpu.VMEM((tm + 2 * HALO, BRANCH_W), F32), pltpu.VMEM((tm + 2 * HALO, 512), F32)],
        compiler_params=_cparams(("arbitrary",)),
        name="ssd_conv",
    )(big, big, big, big, big, big, w1, b1, w2, b2)


def _ssd_direction(xs_ref, bc_ref, sm_ref, smt_ref, dtb_r, dtb_c, alog_r, alog_c, exp_ref, st_ref, d, backward):
    L = CHUNK
    hs = slice(SSD_HEADS * d, SSD_HEADS * (d + 1))
    dt_c = _softplus(sm_ref[...] + dtb_r[...])
    da_c = dt_c * (-jnp.exp(alog_r[...]))
    dt_r = _softplus(smt_ref[hs, :] + dtb_c[hs, :])
    da_r = dt_r * (-jnp.exp(alog_c[hs, :]))
    m_c = _tri(L, lower=not backward).astype(BF16)
    m_r = _tri(L, lower=backward).astype(BF16)
    acs_c = _dot_exact_lhs(m_c, da_c)
    acs_r = _dot_exact_rhs(da_r, m_r)
    valid = _tri(L, lower=not backward)
    acs_last = acs_c[0:1, :] if backward else acs_c[L - 1:L, :]
    w_c = dt_c * jnp.exp(acs_last - acs_c)
    e_c = jnp.exp(acs_c)
    expand = exp_ref[d]

    def widen(v):
        hi = v.astype(BF16)
        lo = (v - hi.astype(F32)).astype(BF16)
        return (jnp.dot(hi, expand, preferred_element_type=F32)
                + jnp.dot(lo, expand, preferred_element_type=F32))

    w_wide = widen(w_c)
    e_wide = widen(e_c)
    cd_wide = e_wide[0:1, :] if backward else e_wide[L - 1:L, :]

    xs = xs_ref[...]
    xs_f = xs.astype(F32)
    lane = lax.broadcasted_iota(jnp.int32, (L, LANES), 1)
    gw = BRANCH_W // SSD_GROUPS
    hpg = SSD_HEADS // SSD_GROUPS
    ys = []
    for g in range(SSD_GROUPS):
        bm = bc_ref[:, SSD_STATE * g:SSD_STATE * (g + 1)]
        cm = bc_ref[:, SSD_GROUPS * SSD_STATE + SSD_STATE * g:SSD_GROUPS * SSD_STATE + SSD_STATE * (g + 1)]
        cb = lax.dot_general(cm, bm, (((1,), (1,)), ((), ())), preferred_element_type=F32)
        state = st_ref[g]
        y_off = jnp.dot(cm, state.astype(BF16), preferred_element_type=F32)
        cols = []
        for hp in range(hpg // 2):
            x_pair = xs[:, gw * g + LANES * hp:gw * g + LANES * (hp + 1)]
            acc = None
            for sub in range(2):
                h = hpg * g + 2 * hp + sub
                hc = SSD_HEADS * d + h
                dm = acs_c[:, hc:hc + 1] - acs_r[h:h + 1, :]
                decay = jnp.exp(jnp.where(valid, dm, -jnp.inf))
                wmat = (cb * decay * dt_r[h:h + 1, :]).astype(BF16)
                keep = (lane < SSD_HEAD_DIM) if sub == 0 else (lane >= SSD_HEAD_DIM)
                xh = jnp.where(keep, x_pair, jnp.zeros_like(x_pair))
                part = jnp.dot(wmat, xh, preferred_element_type=F32)
                acc = part if acc is None else acc + part
            cols.append(acc)
        y_diag = jnp.concatenate(cols, axis=1)
        ys.append(y_diag + y_off * e_wide[:, gw * g:gw * (g + 1)])
        xw = (xs_f[:, gw * g:gw * (g + 1)] * w_wide[:, gw * g:gw * (g + 1)]).astype(BF16)
        new = lax.dot_general(bm, xw, (((0,), (0,)), ((), ())), preferred_element_type=F32)
        st_ref[g] = cd_wide[:, gw * g:gw * (g + 1)] * state + new
    return jnp.concatenate(ys, axis=1), xs_f


def _ssd_kernel(xsf_ref, bcf_ref, smf_ref, smtf_ref, xsb_ref, bcb_ref, smb_ref, smtb_ref,
                dtb_r, dtb_c, alog_r, alog_c, dskip_ref, exp_ref, yf_ref, yb_ref, stf, stb):
    @pl.when(pl.program_id(1) == 0)
    def _():
        stf[...] = jnp.zeros_like(stf)
        stb[...] = jnp.zeros_like(stb)

    y, xs_f = _ssd_direction(xsf_ref, bcf_ref, smf_ref, smtf_ref, dtb_r, dtb_c, alog_r, alog_c, exp_ref, stf,
                             0, False)
    yf_ref[...] = (y + dskip_ref[...] * xs_f).astype(BF16)
    y, _ = _ssd_direction(xsb_ref, bcb_ref, smb_ref, smtb_ref, dtb_r, dtb_c, alog_r, alog_c, exp_ref, stb,
                          1, True)
    yb_ref[...] = y.astype(BF16)


def _ssd_call(xs_c, bc_c, small, small_t, dtb_r, dtb_c, alog_r, alog_c, dskip, expand, batch, seq_len):
    t = xs_c.shape[0]
    L = CHUNK
    nc = seq_len // L
    fwd = lambda b, j: (b * nc + j, 0)
    bwd = lambda b, j: (b * nc + (nc - 1 - j), 0)
    fwd_t = lambda b, j: (0, b * nc + j)
    bwd_t = lambda b, j: (0, b * nc + (nc - 1 - j))
    const = lambda b, j: (0, 0)
    nh2 = 2 * SSD_HEADS
    return pl.pallas_call(
        _ssd_kernel,
        out_shape=(jax.ShapeDtypeStruct((t, BRANCH_W), BF16), jax.ShapeDtypeStruct((t, BRANCH_W), BF16)),
        grid=(batch, nc),
        in_specs=[
            pl.BlockSpec((L, BRANCH_W), fwd), pl.BlockSpec((L, 512), fwd),
            pl.BlockSpec((L, LANES), fwd), pl.BlockSpec((nh2, L), fwd_t),
            pl.BlockSpec((L, BRANCH_W), bwd), pl.BlockSpec((L, 512), bwd),
            pl.BlockSpec((L, LANES), bwd), pl.BlockSpec((nh2, L), bwd_t),
            pl.BlockSpec((1, LANES), const), pl.BlockSpec((nh2, 1), const),
            pl.BlockSpec((1, LANES), const), pl.BlockSpec((nh2, 1), const),
            pl.BlockSpec((1, BRANCH_W), const), pl.BlockSpec((2, LANES, BRANCH_W), lambda b, j: (0, 0, 0)),
        ],
        out_specs=(pl.BlockSpec((L, BRANCH_W), fwd), pl.BlockSpec((L, BRANCH_W), bwd)),
        scratch_shapes=[pltpu.VMEM((SSD_GROUPS, SSD_STATE, 512), F32), pltpu.VMEM((SSD_GROUPS, SSD_STATE, 512), F32)],
        compiler_params=_cparams(("arbitrary", "arbitrary")),
        name="ssd_scan",
    )(xs_c, bc_c, small, small_t, xs_c, bc_c, small, small_t, dtb_r, dtb_c, alog_r, alog_c, dskip, expand)


def _swap32(y):
    lane = lax.broadcasted_iota(jnp.int32, y.shape, 1)
    lo = (lane % 64) < 32
    return jnp.where(lo, pltpu.roll(y, 96, 1), pltpu.roll(y, 32, 1))


def _gqa_prep_kernel(q_ref, k_ref, v_ref, cos_ref, sin_ref, qw_ref, kw_ref, qo_ref, ko_ref, vto_ref, *, scale):
    cos = cos_ref[...]
    sin = sin_ref[...]

    def one(x, w, s):
        x = x.astype(F32)
        ms = jnp.mean(x * x, axis=-1, keepdims=True)
        y = x * lax.rsqrt(ms + EPS) * w
        return ((y * cos + _swap32(y) * sin) * s).astype(BF16)

    for h in range(GQA_HEADS):
        sl = slice(LANES * h, LANES * (h + 1))
        qo_ref[:, sl] = one(q_ref[:, sl], qw_ref[...], scale)
    for h in range(GQA_KV_HEADS):
        sl = slice(LANES * h, LANES * (h + 1))
        ko_ref[h] = one(k_ref[:, sl], kw_ref[...], 1.0)
        vto_ref[sl, :] = v_ref[:, sl].astype(F32).T.astype(BF16)


def _gqa_prep_call(big, cos_t, sin_t, qw, kw, seq_len):
    t = big.shape[0]
    tm = 512
    tps = seq_len // tm
    kern = functools.partial(_gqa_prep_kernel, scale=GQA_HEAD_DIM ** -0.5 * math.log2(math.e))
    kvw = GQA_KV_HEADS * GQA_HEAD_DIM
    return pl.pallas_call(
        kern,
        out_shape=(jax.ShapeDtypeStruct((t, BRANCH_W), BF16), jax.ShapeDtypeStruct((GQA_KV_HEADS, t, LANES), BF16),
                   jax.ShapeDtypeStruct((kvw, t), BF16)),
        grid=(t // tm,),
        in_specs=[
            pl.BlockSpec((tm, BRANCH_W), lambda i: (i, C_GQ // BRANCH_W)),
            pl.BlockSpec((tm, kvw), lambda i: (i, C_GK // kvw)),
            pl.BlockSpec((tm, kvw), lambda i: (i, C_GV // kvw)),
            pl.BlockSpec((tm, LANES), lambda i: (i % tps, 0)),
            pl.BlockSpec((tm, LANES), lambda i: (i % tps, 0)),
            pl.BlockSpec((1, LANES), lambda i: (0, 0)),
            pl.BlockSpec((1, LANES), lambda i: (0, 0)),
        ],
        out_specs=(pl.BlockSpec((tm, BRANCH_W), lambda i: (i, 0)),
                   pl.BlockSpec((GQA_KV_HEADS, tm, LANES), lambda i: (0, i, 0)),
                   pl.BlockSpec((kvw, tm), lambda i: (0, i))),
        compiler_params=_cparams(("arbitrary",)),
        name="gqa_prep",
    )(big, big, big, cos_t, sin_t, qw, kw)


def _na_prep_kernel(q_ref, k_ref, qw_ref, kw_ref, qo_ref, ko_ref, *, scale):
    def one(x, w, s):
        x = x.astype(F32)
        lane = lax.broadcasted_iota(jnp.int32, x.shape, 1)
        lo = lane < NA_HEAD_DIM
        x2 = x * x
        s_lo = jnp.sum(jnp.where(lo, x2, 0.0), axis=-1, keepdims=True)
        s_hi = jnp.sum(jnp.where(lo, 0.0, x2), axis=-1, keepdims=True)
        ms = jnp.where(lo, s_lo, s_hi) * (1.0 / NA_HEAD_DIM)
        return (x * lax.rsqrt(ms + EPS) * w * s).astype(BF16)

    for hp in range(NA_HEADS // 2):
        sl = slice(LANES * hp, LANES * (hp + 1))
        qo_ref[:, sl] = one(q_ref[:, sl], qw_ref[...], scale)
        ko_ref[:, sl] = one(k_ref[:, sl], kw_ref[...], 1.0)


def _na_prep_call(big, qw2, kw2):
    t = big.shape[0]
    tm = 512
    kern = functools.partial(_na_prep_kernel, scale=NA_HEAD_DIM ** -0.5)
    return pl.pallas_call(
        kern,
        out_shape=(jax.ShapeDtypeStruct((t, BRANCH_W), BF16), jax.ShapeDtypeStruct((t, BRANCH_W), BF16)),
        grid=(t // tm,),
        in_specs=[
            pl.BlockSpec((tm, BRANCH_W), lambda i: (i, C_NQ // BRANCH_W)),
            pl.BlockSpec((tm, BRANCH_W), lambda i: (i, C_NK // BRANCH_W)),
            pl.BlockSpec((1, LANES), lambda i: (0, 0)),
            pl.BlockSpec((1, LANES), lambda i: (0, 0)),
        ],
        out_specs=(pl.BlockSpec((tm, BRANCH_W), lambda i: (i, 0)), pl.BlockSpec((tm, BRANCH_W), lambda i: (i, 0))),
        compiler_params=_cparams(("arbitrary",)),
        name="na_prep",
    )(big, big, qw2, kw2)


FLASH_DEN_ROWS = 16


def _flash_kernel(q_ref, k_ref, vt_ref, o_ref, m_scr, acc, *, tq, tk, grp, nkb, unroll):
    m_scr[...] = jnp.full(m_scr.shape, -jnp.inf, F32)
    acc[...] = jnp.zeros(acc.shape, F32)
    ones_rows = jnp.ones((FLASH_DEN_ROWS, tk), BF16)

    def body(it, carry):
        units = [(j, h) for j in range(unroll) for h in range(grp)]

        def kv_start(j):
            return pl.multiple_of((it * unroll + j) * tk, tk)

        def scores(u):
            j, h = units[u]
            kblk = k_ref[pl.ds(kv_start(j), tk), :]
            qh = q_ref[:, LANES * h:LANES * (h + 1)]
            return lax.dot_general(kblk, qh, (((1,), (1,)), ((), ())), preferred_element_type=F32)

        def softmax(u, st):
            h = units[u][1]
            m_old = m_scr[h]
            m_new = jnp.maximum(m_old, jnp.max(st, axis=0, keepdims=True))
            alpha = jnp.exp2(m_old - m_new)
            p = jnp.exp2(st - m_new)
            m_scr[h] = m_new
            return alpha, p.astype(BF16)

        def accumulate(u, alpha, p):
            j, h = units[u]
            vt = jnp.concatenate([vt_ref[:, pl.ds(kv_start(j), tk)], ones_rows], axis=0)
            acc[h] = alpha * acc[h] + jnp.dot(vt, p, preferred_element_type=F32)

        n = len(units)
        lead = 2
        st = {u: scores(u) for u in range(min(lead, n))}
        pending = None
        for u in range(n):
            cur = softmax(u, st.pop(u))
            if u + lead < n:
                st[u + lead] = scores(u + lead)
            if pending is not None:
                accumulate(u - 1, *pending)
            pending = cur
        accumulate(n - 1, *pending)
        return carry

    lax.fori_loop(0, nkb // unroll, body, 0)
    for h in range(grp):
        a = acc[h]
        o = (a[:LANES, :] / a[LANES:LANES + 1, :]).T
        o_ref[:, LANES * h:LANES * (h + 1)] = o.astype(BF16)


def _flash_call(qp, kp, vt, batch, seq_len):
    t = qp.shape[0]
    grp = GQA_HEADS // GQA_KV_HEADS
    tq, tk = 256, 512
    nq = seq_len // tq
    gw = grp * LANES
    nkb = seq_len // tk
    unroll = math.gcd(nkb, 8)
    kern = functools.partial(_flash_kernel, tq=tq, tk=tk, grp=grp, nkb=nkb, unroll=unroll)
    return pl.pallas_call(
        kern,
        out_shape=jax.ShapeDtypeStruct((t, BRANCH_W), BF16),
        grid=(batch, GQA_KV_HEADS, nq),
        in_specs=[
            pl.BlockSpec((tq, gw), lambda b, g, qi: (b * nq + qi, g)),
            pl.BlockSpec((None, seq_len, LANES), lambda b, g, qi: (g, b, 0)),
            pl.BlockSpec((LANES, seq_len), lambda b, g, qi: (g, b)),
        ],
        out_specs=pl.BlockSpec((tq, gw), lambda b, g, qi: (b * nq + qi, g)),
        scratch_shapes=[pltpu.VMEM((grp, 1, tq), F32), pltpu.VMEM((grp, LANES + FLASH_DEN_ROWS, tq), F32)],
        compiler_params=_cparams(("arbitrary", "arbitrary", "arbitrary")),
        name="gqa_flash",
    )(qp, kp, vt)


NA_ROWS_PER_STEP = 4


def _na_key_rows(r):
    kr = r + NA_WIN_ROWS - 1
    return kr + (kr * GRID_W % LANES) // GRID_W


NA_UNITS_PER_STEP = 4


def _na_kernel(q_ref, k_ref, v_ref, tbl_ref, o_ref, *, rows, R, U):
    KR = _na_key_rows(R)
    W = GRID_W
    lane = lax.broadcasted_iota(jnp.int32, (W, LANES), 1)
    head0 = lane < NA_HEAD_DIM
    zero = jnp.zeros((W, LANES), BF16)

    def row0(u):
        return (pl.program_id(2) * U + u) * R

    def key_row0(u):
        return jnp.clip(row0(u) - NA_WIN_ROWS // 2, 0, rows - KR)

    def key_slice(u):
        return pl.ds(pl.multiple_of(key_row0(u) * W, W), KR * W)

    def scores(u):
        kblk = k_ref[key_slice(u), :]
        parts = []
        for qr in range(R):
            qrow = q_ref[W * (R * u + qr):W * (R * u + qr + 1), :]
            parts.append(jnp.where(head0, qrow, zero))
            parts.append(jnp.where(head0, zero, qrow))
        qcat = jnp.concatenate(parts, axis=0)
        return lax.dot_general(kblk, qcat, (((1,), (1,)), ((), ())), preferred_element_type=F32)

    def softmax(u, st):
        r0, ks = row0(u), key_row0(u)
        blocks = []
        for a in range(KR):
            rowb = []
            for qr in range(R):
                krow = ks + a
                r = r0 + qr
                rs = jnp.clip(r - NA_WIN_ROWS // 2, 0, rows - NA_WIN_ROWS)
                ok = jnp.logical_and(krow >= rs, krow < rs + NA_WIN_ROWS)
                idx = jnp.where(ok, krow - r + (NA_WIN_ROWS - 1), 2 * NA_WIN_ROWS - 1)
                rowb.append(st[W * a:W * (a + 1), LANES * qr:LANES * (qr + 1)] + tbl_ref[idx])
            blocks.append(jnp.concatenate(rowb, axis=1))
        s = jnp.concatenate(blocks, axis=0)
        m = jnp.max(s, axis=0, keepdims=True)
        return jnp.exp(s - m).astype(BF16)

    def values(u, p):
        vblk = v_ref[key_slice(u), :]
        vaug = jnp.concatenate([vblk, jnp.ones_like(vblk)], axis=1)
        o = lax.dot_general(p, vaug, (((0,), (0,)), ((), ())), preferred_element_type=F32)
        o = o[:, :LANES] / o[:, LANES:]
        for qr in range(R):
            top = o[2 * W * qr:2 * W * qr + W, :]
            bot = o[2 * W * qr + W:2 * W * (qr + 1), :]
            o_ref[W * (R * u + qr):W * (R * u + qr + 1), :] = jnp.where(head0, top, bot).astype(BF16)

    st = {u: scores(u) for u in range(min(2, U))}
    pending = None
    for u in range(U):
        p = softmax(u, st.pop(u))
        if u + 2 < U:
            st[u + 2] = scores(u + 2)
        if pending is not None:
            values(u - 1, pending)
        pending = p
    values(U - 1, pending)


def _na_call(qn, kn, big, tbl, batch, seq_len):
    t = qn.shape[0]
    rows = seq_len // GRID_W
    R, U = NA_ROWS_PER_STEP, NA_UNITS_PER_STEP
    nst = rows // (R * U)
    kern = functools.partial(_na_kernel, rows=rows, R=R, U=U)
    return pl.pallas_call(
        kern,
        out_shape=jax.ShapeDtypeStruct((t, BRANCH_W), BF16),
        grid=(batch, NA_HEADS // 2, nst),
        in_specs=[
            pl.BlockSpec((U * R * GRID_W, LANES), lambda b, hp, i: (b * nst + i, hp)),
            pl.BlockSpec((seq_len, LANES), lambda b, hp, i: (b, hp)),
            pl.BlockSpec((seq_len, LANES), lambda b, hp, i: (b, C_NV // LANES + hp)),
            pl.BlockSpec((None, 2 * NA_WIN_ROWS, GRID_W, LANES), lambda b, hp, i: (hp, 0, 0, 0)),
        ],
        out_specs=pl.BlockSpec((U * R * GRID_W, LANES), lambda b, hp, i: (b * nst + i, hp)),
        compiler_params=_cparams(("arbitrary", "arbitrary", "arbitrary")),
        name="na_attn",
    )(qn, kn, big, tbl)


def _mlstm_gates(sm_ref, smt_ref, gb_r, gb_c, backward):
    L = CHUNK
    g_c = sm_ref[...] + gb_r[...]
    g_r = smt_ref[...] + gb_c[...]
    m_c = _tri(L, lower=not backward).astype(BF16)
    m_r = _tri(L, lower=backward).astype(BF16)
    bcs_c = _dot_exact_lhs(m_c, -_softplus(-g_c))
    bcs_r = _dot_exact_rhs(-_softplus(-g_r), m_r)
    return g_c, g_r, bcs_c, bcs_r, _tri(L, lower=not backward)


def _mlstm_unit(refs, gates, d, h, backward):
    q_ref, k_ref, v_ref, c_scr, n_scr, m_scr, o_ref = refs
    g_c, g_r, bcs_c_all, bcs_r_all, valid = gates
    L = CHUNK
    ii = 2 * ML_HEADS * d + h
    fi = ii + ML_HEADS
    sl = slice(ML_HEAD_DIM * h, ML_HEAD_DIM * (h + 1))
    t = {}

    def stage_a():
        t["qh"] = q_ref[:, sl]
        t["kh"] = k_ref[:, sl] * (ML_HEAD_DIM ** -0.5)
        t["sc"] = lax.dot_general(t["qh"], t["kh"], (((1,), (1,)), ((), ())), preferred_element_type=F32)
        t["c_prev"] = c_scr[h]
        t["qc"] = jnp.dot(t["qh"], t["c_prev"].astype(BF16), preferred_element_type=F32)

    def stage_b():
        li_c = g_c[:, S_MG + ii:S_MG + ii + 1]
        li_r = g_r[ii:ii + 1, :]
        bcs_c = bcs_c_all[:, S_MG + fi:S_MG + fi + 1]
        bcs_r = bcs_r_all[fi:fi + 1, :]
        b_end = bcs_c[0:1, :] if backward else bcs_c[L - 1:L, :]
        dm = jnp.where(valid, bcs_c - (bcs_r - li_r), -jnp.inf)
        m_prev = m_scr[h, 0:1, 0:1]
        inter = bcs_c + m_prev
        m_t = jnp.maximum(inter, jnp.max(dm, axis=1, keepdims=True))
        w_intra = jnp.exp(dm - m_t) * t.pop("sc")
        t["w_inter"] = jnp.exp(inter - m_t)
        t["n_prev"] = n_scr[h, 0:1, :]
        den = (jnp.sum(w_intra, axis=1, keepdims=True)
               + t["w_inter"] * jnp.sum(t["qh"].astype(F32) * t["n_prev"], axis=1, keepdims=True))
        t["den"] = jnp.maximum(jnp.abs(den), jnp.exp(-m_t))
        t["w_intra"] = w_intra.astype(BF16)
        g_end_r = b_end - bcs_r + li_r
        m_loc = jnp.max(g_end_r, axis=1, keepdims=True)
        w_end_c = jnp.exp(b_end - bcs_c + li_c - m_loc)
        kw = t.pop("kh").astype(F32) * w_end_c
        t["n_loc"] = jnp.sum(kw, axis=0, keepdims=True)
        t["kw"] = kw.astype(BF16)
        m_new = jnp.maximum(b_end + m_prev, m_loc)
        t["a_old"] = jnp.exp(b_end + m_prev - m_new)
        t["a_loc"] = jnp.exp(m_loc - m_new)
        t["m_new"] = m_new

    def stage_c():
        vh = v_ref[:, sl]
        t["wv"] = jnp.dot(t.pop("w_intra"), vh, preferred_element_type=F32)
        t["c_loc"] = lax.dot_general(t.pop("kw"), vh, (((0,), (0,)), ((), ())), preferred_element_type=F32)

    def stage_d():
        num = t["wv"] + t["w_inter"] * t["qc"]
        o_ref[:, sl] = (num / t["den"]).astype(BF16)
        c_scr[h] = t["a_old"] * t["c_prev"] + t["a_loc"] * t["c_loc"]
        n_scr[h] = jnp.broadcast_to(t["a_old"] * t["n_prev"] + t["a_loc"] * t["n_loc"], n_scr.shape[1:])
        m_scr[h] = jnp.broadcast_to(t["m_new"], m_scr.shape[1:])

    return stage_a, stage_b, stage_c, stage_d


def _mlstm_kernel(qf, kf, vf, smf, smtf, qb, kb, vb, smb, smtb, gb_r, gb_c, hf_ref, hb_ref,
                  cf, nf, mf, cb, nb, mb):
    @pl.when(pl.program_id(1) == 0)
    def _():
        for r in (cf, nf, mf, cb, nb, mb):
            r[...] = jnp.zeros_like(r)

    gates_f = _mlstm_gates(smf, smtf, gb_r, gb_c, False)
    gates_b = _mlstm_gates(smb, smtb, gb_r, gb_c, True)
    units = []
    for h in range(ML_HEADS):
        units.append(_mlstm_unit((qf, kf, vf, cf, nf, mf, hf_ref), gates_f, 0, h, False))
        units.append(_mlstm_unit((qb, kb, vb, cb, nb, mb, hb_ref), gates_b, 1, h, True))
    n = len(units)
    for u in range(min(2, n)):
        units[u][0]()
    for u in range(n):
        units[u][1]()
        if u + 2 < n:
            units[u + 2][0]()
        units[u][2]()
        if u >= 1:
            units[u - 1][3]()
    units[n - 1][3]()


def _mlstm_call(big, small, small_t, gb_r, gb_c, batch, seq_len):
    t = big.shape[0]
    L = CHUNK
    nc = seq_len // L
    cq, ck, cv = C_MQ // BRANCH_W, C_MK // BRANCH_W, C_MV // BRANCH_W
    f = lambda c: (lambda b, j: (b * nc + j, c))
    r = lambda c: (lambda b, j: (b * nc + (nc - 1 - j), c))
    const = lambda b, j: (0, 0)
    state = [pltpu.VMEM((ML_HEADS, ML_HEAD_DIM, ML_HEAD_DIM), F32), pltpu.VMEM((ML_HEADS, 8, ML_HEAD_DIM), F32),
             pltpu.VMEM((ML_HEADS, 8, LANES), F32)]
    return pl.pallas_call(
        _mlstm_kernel,
        out_shape=(jax.ShapeDtypeStruct((t, BRANCH_W), BF16), jax.ShapeDtypeStruct((t, BRANCH_W), BF16)),
        grid=(batch, nc),
        in_specs=[
            pl.BlockSpec((L, BRANCH_W), f(cq)), pl.BlockSpec((L, BRANCH_W), f(ck)), pl.BlockSpec((L, BRANCH_W), f(cv)),
            pl.BlockSpec((L, LANES), f(0)), pl.BlockSpec((16, L), lambda b, j: (S_MG // 16, b * nc + j)),
            pl.BlockSpec((L, BRANCH_W), r(cq)), pl.BlockSpec((L, BRANCH_W), r(ck)), pl.BlockSpec((L, BRANCH_W), r(cv)),
            pl.BlockSpec((L, LANES), r(0)), pl.BlockSpec((16, L), lambda b, j: (S_MG // 16, b * nc + (nc - 1 - j))),
            pl.BlockSpec((1, LANES), const), pl.BlockSpec((16, 1), const),
        ],
        out_specs=(pl.BlockSpec((L, BRANCH_W), f(0)), pl.BlockSpec((L, BRANCH_W), r(0))),
        scratch_shapes=state + state,
        compiler_params=_cparams(("arbitrary", "arbitrary")),
        name="mlstm_scan",
    )(big, big, big, small, small_t, big, big, big, small, small_t, gb_r, gb_c)


def _merge_kernel(yf_ref, yb_ref, z_ref, og_ref, on_ref, hf_ref, hb_ref, mo_ref, gate_ref, snw_ref, mnw_ref,
                  wb_ref, o_ref):
    y = (yf_ref[...].astype(F32) + yb_ref[...].astype(F32)) * _silu(z_ref[...].astype(F32))
    ms = jnp.mean(y * y, axis=-1, keepdims=True)
    y0 = (y * lax.rsqrt(ms + EPS) * snw_ref[...]).astype(BF16)
    parts = []
    for h in range(ML_HEADS):
        sl = slice(ML_HEAD_DIM * h, ML_HEAD_DIM * (h + 1))
        hh = hf_ref[:, sl].astype(F32) + hb_ref[:, sl].astype(F32)
        ms = jnp.mean(hh * hh, axis=-1, keepdims=True)
        parts.append(hh * lax.rsqrt(ms + EPS) * mnw_ref[:, sl])
    y3 = (jax.nn.sigmoid(mo_ref[...].astype(F32)) * jnp.concatenate(parts, axis=1)).astype(BF16)
    acc = None
    for i, yi in enumerate((y0, og_ref[...], on_ref[...], y3)):
        g = gate_ref[:, D_MODEL * i:D_MODEL * (i + 1)].astype(F32)
        term = g * jnp.dot(yi, wb_ref[i], preferred_element_type=F32)
        acc = term if acc is None else acc + term
    o_ref[...] = acc.astype(BF16)


def _merge_call(yf, yb, og, on, hf, hb, big, snw, mnw, wb):
    t = big.shape[0]
    tm = 256
    row = lambda i: (i, 0)
    const2 = lambda i: (0, 0)
    return pl.pallas_call(
        _merge_kernel,
        out_shape=jax.ShapeDtypeStruct((t, D_MODEL), BF16),
        grid=(t // tm,),
        in_specs=[
            pl.BlockSpec((tm, BRANCH_W), row), pl.BlockSpec((tm, BRANCH_W), row),
            pl.BlockSpec((tm, BRANCH_W), lambda i: (i, C_Z // BRANCH_W)),
            pl.BlockSpec((tm, BRANCH_W), row), pl.BlockSpec((tm, BRANCH_W), row),
            pl.BlockSpec((tm, BRANCH_W), row), pl.BlockSpec((tm, BRANCH_W), row),
            pl.BlockSpec((tm, BRANCH_W), lambda i: (i, C_MO // BRANCH_W)),
            pl.BlockSpec((tm, N_BRANCH * D_MODEL), lambda i: (i, 0)),
            pl.BlockSpec((1, BRANCH_W), const2), pl.BlockSpec((1, BRANCH_W), const2),
            pl.BlockSpec((N_BRANCH, BRANCH_W, D_MODEL), lambda i: (0, 0, 0), pipeline_mode=pl.Buffered(1)),
        ],
        out_specs=pl.BlockSpec((tm, D_MODEL), row),
        compiler_params=_cparams(("arbitrary",)),
        name="merge",
    )(yf, yb, big, og, on, hf, hb, big, big, snw, mnw, wb)


def _resproj_kernel(a_ref, w_ref, x_ref, mod_ref, o_ref, *, mod_row):
    acc = jnp.dot(a_ref[...], w_ref[...], preferred_element_type=F32)
    o_ref[...] = x_ref[...] + mod_ref[mod_row:mod_row + 1, :] * acc


def _resproj_call(a, w, x, mod, mod_row, seq_len, tm, tn):
    t, k = a.shape
    n = w.shape[1]
    tiles_per_seq = seq_len // tm
    kern = functools.partial(_resproj_kernel, mod_row=mod_row)
    return pl.pallas_call(
        kern,
        out_shape=jax.ShapeDtypeStruct((t, n), F32),
        grid=(n // tn, t // tm),
        in_specs=[
            pl.BlockSpec((tm, k), lambda j, i: (i, 0)),
            pl.BlockSpec((k, tn), lambda j, i: (0, j)),
            pl.BlockSpec((tm, tn), lambda j, i: (i, j)),
            pl.BlockSpec((None, 6, tn), lambda j, i: (i // tiles_per_seq, 0, j)),
        ],
        out_specs=pl.BlockSpec((tm, tn), lambda j, i: (i, j)),
        compiler_params=_cparams(("arbitrary", "arbitrary")),
        name="resproj",
    )(a, w, x, mod)


def _ffn_in_kernel(x_ref, mod_ref, nw_ref, wu_ref, wg_ref, o_ref, h_scr, *, tm, rc):
    @pl.when(pl.program_id(1) == 0)
    def _():
        nw = nw_ref[...]
        sh = mod_ref[3:4, :]
        sc = 1.0 + mod_ref[4:5, :]

        def body(r, carry):
            r0 = pl.multiple_of(r * rc, rc)
            h_scr[pl.ds(r0, rc), :] = _norm_mod_rows(x_ref, nw, sc, sh, r0, rc).astype(BF16)
            return carry

        lax.fori_loop(0, tm // rc, body, 0)

    h = h_scr[...]
    up = jnp.dot(h, wu_ref[...], preferred_element_type=F32)
    gate = jnp.dot(h, wg_ref[...], preferred_element_type=F32)
    o_ref[...] = (_silu(gate) * up).astype(BF16)


def _ffn_in_call(x, mod, nw, w_ffn_in, seq_len):
    t = x.shape[0]
    tm, rc = 1024, 128
    tn = FFN_TN
    nt = FFN_HIDDEN // tn
    tiles_per_seq = seq_len // tm
    kern = functools.partial(_ffn_in_kernel, tm=tm, rc=rc)
    return pl.pallas_call(
        kern,
        out_shape=jax.ShapeDtypeStruct((t, FFN_HIDDEN), BF16),
        grid=(t // tm, nt),
        in_specs=[
            pl.BlockSpec((tm, D_MODEL), lambda i, j: (i, 0)),
            pl.BlockSpec((None, 6, D_MODEL), lambda i, j: (i // tiles_per_seq, 0, 0)),
            pl.BlockSpec((1, D_MODEL), lambda i, j: (0, 0)),
            pl.BlockSpec((D_MODEL, tn), lambda i, j: (0, j)),
            pl.BlockSpec((D_MODEL, tn), lambda i, j: (0, nt + j)),
        ],
        out_specs=pl.BlockSpec((tm, tn), lambda i, j: (i, j)),
        scratch_shapes=[pltpu.VMEM((tm, D_MODEL), BF16)],
        compiler_params=_cparams(("arbitrary", "arbitrary")),
        name="ffn_in",
    )(x, mod, nw, w_ffn_in, w_ffn_in)


def _rope_tables(seq_len):
    pos = np.arange(seq_len)
    rows, cols = pos // GRID_W, pos % GRID_W
    d = GQA_HEAD_DIM // 2
    inv = np.asarray(ROPE_THETA ** (-np.arange(0, d, 2, dtype=np.float32) / d), dtype=np.float32)

    def cs(p):
        ang = p[:, None].astype(np.float32) * inv[None]
        return np.cos(ang), np.sin(ang)

    cr, sr = cs(rows)
    cc, sc = cs(cols)
    cos = np.concatenate([cr, cr, cc, cc], axis=1).astype(np.float32)
    sin = np.concatenate([-sr, sr, -sc, sc], axis=1).astype(np.float32)
    return jnp.asarray(cos), jnp.asarray(sin)


def _na_bias_table(rpb):
    c = np.arange(GRID_W)[:, None]
    j = np.arange(GRID_W)[None, :]
    start = np.clip(j - NA_WIN_COLS // 2, 0, GRID_W - NA_WIN_COLS)
    ok = (c >= start) & (c < start + NA_WIN_COLS)
    dc = np.clip(c - j + (NA_WIN_COLS - 1), 0, 2 * NA_WIN_COLS - 2)
    t = rpb[:, :, dc]
    t = jnp.where(jnp.asarray(ok)[None, None], t, NEG)
    t = t.reshape(NA_HEADS // 2, 2, 2 * NA_WIN_ROWS - 1, GRID_W, GRID_W)
    t = t.transpose(0, 2, 3, 1, 4).reshape(NA_HEADS // 2, 2 * NA_WIN_ROWS - 1, GRID_W, LANES)
    pad = jnp.full((NA_HEADS // 2, 1, GRID_W, LANES), NEG, F32)
    return jnp.concatenate([t, pad], axis=1)


def _lane_row(v, offset):
    return jnp.zeros((1, LANES), F32).at[0, offset:offset + v.shape[0]].set(v)


INPROJ_TN = 1024
FFN_TN = 512


def _layer_weights(l, w_in, w_bgate, b_bgate):
    o = np.cumsum((0, 1024, 1536, 32, 1024, 256, 256, 1024, 1024, 1024, 1024, 1024, 1024, 1024, 16))
    cols = lambda w: [w[:, o[i]:o[i + 1]] for i in range(14)]
    z, xbc, _, gq, gk, gv, nq, nk, nv, mq, mk, mv, mo, _ = cols(w_in[l].astype(BF16))
    w_big = jnp.concatenate([w_bgate[l].astype(BF16), z, xbc[:, :1024], gq, nq, nk, nv, mq, mk, mv, mo,
                             xbc[:, 1024:], gk, gv], axis=1)
    b_big = jnp.concatenate([b_bgate[l], jnp.zeros((BIG_COLS - N_BRANCH * D_MODEL,), F32)])[None, :]
    dt, mg = w_in[l][:, o[2]:o[3]], w_in[l][:, o[13]:o[14]]
    ws = jnp.concatenate([dt, mg, jnp.zeros((D_MODEL, LANES - 48), F32)], axis=1)
    ws_hi = ws.astype(BF16)
    ws_lo = (ws - ws_hi.astype(F32)).astype(BF16)
    return w_big, b_big, ws_hi, ws_lo


def _trunk_group(x, mods, seq_len, batch, consts, layers):
    for l in range(DEPTH):
        lw = layers[l]
        mod = mods[l]
        big, small = _inproj_call(x, mod, lw["norm1"], lw["w_big"], lw["b_big"], lw["ws_hi"], lw["ws_lo"], seq_len)
        small_t = small.T
        xs_c, bc_c = _conv_call(big, lw["conv_w1"], lw["conv_b1"], lw["conv_w2"], lw["conv_b2"], seq_len)
        yf, yb = _ssd_call(xs_c, bc_c, small, small_t, lw["dtb_r"], lw["dtb_c"], lw["alog_r"], lw["alog_c"],
                           lw["dskip"], consts["expand"], batch, seq_len)
        qp, kp, vt = _gqa_prep_call(big, consts["cos"][seq_len], consts["sin"][seq_len], lw["gqa_qw"], lw["gqa_kw"],
                                    seq_len)
        og = _flash_call(qp, kp, vt, batch, seq_len)
        qn, kn = _na_prep_call(big, lw["na_qw"], lw["na_kw"])
        on = _na_call(qn, kn, big, lw["na_tbl"], batch, seq_len)
        hf, hb = _mlstm_call(big, small, small_t, lw["ml_gb_r"], lw["ml_gb_c"], batch, seq_len)
        merged = _merge_call(yf, yb, og, on, hf, hb, big, lw["ssd_nw"], lw["ml_nw"], lw["wb"])
        x = _resproj_call(merged, lw["w_o"], x, mod, 2, seq_len, 256, D_MODEL)
        act = _ffn_in_call(x, mod, lw["norm2"], lw["w_ffn_in"], seq_len)
        x = _resproj_call(act, lw["w_ffn_out"], x, mod, 5, seq_len, 512, 1024)
    return x


def kernel(x_prompt, x_sample, c_prompt, c_sample, w_ada, b_ada, norm1_w, w_in, w_bgate, b_bgate, ssd_conv_w,
           ssd_conv_b, ssd_dt_bias, ssd_a_log, ssd_d, ssd_norm_w, gqa_q_norm, gqa_k_norm, na_q_norm, na_k_norm,
           na_rpb, ml_gate_bias, ml_out_norm, w_branch_out, w_o, norm2_w, w_ffn_in, w_ffn_out):
    bp, sp, _ = x_prompt.shape
    bs, ss, _ = x_sample.shape
    nseq = bp + bs
    rows = -(-nseq // 16) * 16
    c_all = jnp.concatenate([c_prompt, c_sample, jnp.zeros((rows - nseq, D_MODEL), F32)], axis=0)
    mods = _ada_call(c_all, w_ada.astype(BF16), b_ada.reshape(DEPTH, 1, 6 * D_MODEL))
    mods = mods.reshape(DEPTH, rows, 6, D_MODEL)

    expand = np.zeros((2, LANES, BRANCH_W), np.float32)
    for d in range(2):
        for h in range(SSD_HEADS):
            expand[d, SSD_HEADS * d + h, SSD_HEAD_DIM * h:SSD_HEAD_DIM * (h + 1)] = 1.0
    consts = {"expand": jnp.asarray(expand, BF16), "cos": {}, "sin": {}}
    for s in {sp, ss}:
        consts["cos"][s], consts["sin"][s] = _rope_tables(s)

    layers = []
    for l in range(DEPTH):
        w_big, b_big, ws_hi, ws_lo = _layer_weights(l, w_in, w_bgate, b_bgate)
        layers.append({
            "norm1": norm1_w[l][None, :], "norm2": norm2_w[l][None, :],
            "w_big": w_big, "b_big": b_big, "ws_hi": ws_hi, "ws_lo": ws_lo,
            "conv_w1": ssd_conv_w[l][:, :BRANCH_W], "conv_b1": ssd_conv_b[l][None, :BRANCH_W],
            "conv_w2": ssd_conv_w[l][:, BRANCH_W:], "conv_b2": ssd_conv_b[l][None, BRANCH_W:],
            "dtb_r": _lane_row(ssd_dt_bias[l].reshape(-1), S_DT), "dtb_c": ssd_dt_bias[l].reshape(2 * SSD_HEADS, 1),
            "alog_r": _lane_row(ssd_a_log[l].reshape(-1), S_DT), "alog_c": ssd_a_log[l].reshape(2 * SSD_HEADS, 1),
            "dskip": jnp.repeat(ssd_d[l], SSD_HEAD_DIM)[None, :],
            "ssd_nw": ssd_norm_w[l][None, :],
            "gqa_qw": gqa_q_norm[l][None, :], "gqa_kw": gqa_k_norm[l][None, :],
            "na_qw": jnp.tile(na_q_norm[l], 2)[None, :], "na_kw": jnp.tile(na_k_norm[l], 2)[None, :],
            "na_tbl": _na_bias_table(na_rpb[l]),
            "ml_gb_r": _lane_row(ml_gate_bias[l].reshape(-1), S_MG), "ml_gb_c": ml_gate_bias[l].reshape(16, 1),
            "ml_nw": ml_out_norm[l][None, :],
            "wb": w_branch_out[l].astype(BF16), "w_o": w_o[l].astype(BF16),
            "w_ffn_in": w_ffn_in[l].astype(BF16), "w_ffn_out": w_ffn_out[l].astype(BF16),
        })

    y_p = _trunk_group(x_prompt.reshape(bp * sp, D_MODEL), mods[:, :bp], sp, bp, consts, layers)
    y_s = _trunk_group(x_sample.reshape(bs * ss, D_MODEL), mods[:, bp:bp + bs], ss, bs, consts, layers)
    return y_p.reshape(bp, sp, D_MODEL), y_s.reshape(bs, ss, D_MODEL)
```
